```python
import jax, jax.numpy as jnp
from jax import lax
import numpy as np

D_MODEL = 2048
BATCH = 4
SEQ = 2048
DEPTH = 4
DEC_BATCH = 8
DEC_SEQ = 4
PAST_LEN = 16384
PAGE_SIZE = 128

N_MIXERS = 2
N_ATTN_LAYERS = (DEPTH + 1) // 2
N_MLSTM_LAYERS = DEPTH // 2
N_HEADS = 16
HEAD_DIM = D_MODEL // N_HEADS
MOBA_BLOCK = 256
MOBA_TOP_K = 3
Q_CHUNK = 32
ROPE_THETA = 10000.0
M_HEADS = 8
M_QK_DIM = D_MODEL // (2 * M_HEADS)
M_V_DIM = D_MODEL // M_HEADS
M_CHUNK = 64
IGATE_CAP = 15.0
D_FF = ((8 * D_MODEL // 3 + 255) // 256) * 256
EPS = 1e-6

kernel_name = 'moba_mlstm_macaron_step'


def rms_norm(x, g):
    xf = x.astype(jnp.float32)
    y = xf * lax.rsqrt(jnp.mean(xf * xf, axis=-1, keepdims=True) + EPS)
    return (y * g.astype(jnp.float32)).astype(x.dtype)


def rope(x, pos):
    half = HEAD_DIM // 2
    inv = ROPE_THETA ** (-jnp.arange(half, dtype=jnp.float32) / half)
    ang = pos.astype(jnp.float32)[:, None] * inv[None, :]
    cos = jnp.cos(ang)[:, None, :]
    sin = jnp.sin(ang)[:, None, :]
    xf = x.astype(jnp.float32)
    x1, x2 = xf[..., :half], xf[..., half:]
    return jnp.concatenate([x1 * cos - x2 * sin, x2 * cos + x1 * sin], axis=-1).astype(x.dtype)


def swiglu(x, w_up, w_down):
    g, u = jnp.split(x @ w_up, 2, axis=-1)
    return (jax.nn.silu(g) * u) @ w_down


def attn_qkv(xn, w_qkv, g_q, g_k, pos):
    n, t, _ = xn.shape
    q, k, v = jnp.split(xn @ w_qkv, 3, axis=-1)
    shp = (n, t, N_HEADS, HEAD_DIM)
    q = rope(rms_norm(q.reshape(shp), g_q), pos)
    k = rope(rms_norm(k.reshape(shp), g_k), pos)
    return q, k, v.reshape(shp)


def moba_core(q, k_own, v_own, own_mask, k_sel, v_sel, sel_valid):
    h, nq, _ = q.shape
    n_own = k_own.shape[1]
    n_sel = k_sel.shape[2]
    scale = HEAD_DIM ** -0.5
    s_own = jnp.einsum('hqd,hld->hql', q, k_own, preferred_element_type=jnp.float32) * scale
    s_own = jnp.where(own_mask[None], s_own, -jnp.inf)
    s_sel = jnp.einsum('hqd,hqkld->hqkl', q, k_sel, preferred_element_type=jnp.float32) * scale
    s_sel = jnp.where(sel_valid[None, :, :, None], s_sel, -jnp.inf)
    p = jax.nn.softmax(jnp.concatenate([s_own, s_sel.reshape(h, nq, n_sel * MOBA_BLOCK)], axis=-1), axis=-1)
    p_own = p[..., :n_own].astype(v_own.dtype)
    p_sel = p[..., n_own:].reshape(h, nq, n_sel, MOBA_BLOCK).astype(v_sel.dtype)
    out = (jnp.einsum('hql,hld->hqd', p_own, v_own, preferred_element_type=jnp.float32)
           + jnp.einsum('hqkl,hqkld->hqd', p_sel, v_sel, preferred_element_type=jnp.float32))
    return out.astype(q.dtype)


def moba_prompt(q, k, v):
    b_, s_ = q.shape[:2]
    nb = -(-s_ // MOBA_BLOCK)
    pad = nb * MOBA_BLOCK - s_

    def blocks(a):
        a = jnp.pad(a, ((0, 0), (0, pad), (0, 0), (0, 0)))
        return a.reshape(b_, nb, MOBA_BLOCK, N_HEADS, HEAD_DIM).transpose(0, 3, 1, 2, 4)

    kb, vb = blocks(k), blocks(v)
    qh = q.transpose(0, 2, 1, 3)
    qblk = jnp.arange(s_) // MOBA_BLOCK
    n_sel = min(MOBA_TOP_K, nb - 1)
    if n_sel > 0:
        kmean = jnp.mean(kb.astype(jnp.float32), axis=3)
        gate = jnp.einsum('bhsd,bhnd->bhsn', qh.astype(jnp.float32), kmean)
        gate = jnp.where(jnp.arange(nb)[None, :] < qblk[:, None], gate, -jnp.inf)
        _, sel = lax.top_k(gate, n_sel)
    else:
        sel = jnp.zeros((b_, N_HEADS, s_, 0), jnp.int32)
    nqc = s_ // Q_CHUNK

    def chunks(a):
        x = a.shape[-1]
        return a.reshape(b_, N_HEADS, nqc, Q_CHUNK, x).transpose(0, 2, 1, 3, 4).reshape(b_ * nqc, N_HEADS, Q_CHUNK, x)

    b_idx = jnp.repeat(jnp.arange(b_), nqc)
    c_idx = jnp.tile(jnp.arange(nqc), b_)
    heads = jnp.arange(N_HEADS)[:, None, None]
    offs = jnp.arange(Q_CHUNK)
    brow = jnp.arange(MOBA_BLOCK)

    def one(args):
        bi, ci, qc, sc = args
        q0 = ci * Q_CHUNK
        blk = q0 // MOBA_BLOCK
        k_own = kb[bi, :, blk]
        v_own = vb[bi, :, blk]
        own_mask = (blk * MOBA_BLOCK + brow)[None, :] <= (q0 + offs)[:, None]
        k_sel = kb[bi, heads, sc]
        v_sel = vb[bi, heads, sc]
        valid = jnp.broadcast_to(jnp.arange(n_sel)[None, :] < blk, (Q_CHUNK, n_sel))
        return moba_core(qc, k_own, v_own, own_mask, k_sel, v_sel, valid)

    out = lax.map(one, (b_idx, c_idx, chunks(qh), chunks(sel)))
    return out.reshape(b_, nqc, N_HEADS, Q_CHUNK, HEAD_DIM).transpose(0, 1, 3, 2, 4).reshape(b_, s_, N_HEADS, HEAD_DIM)


def moba_sample(q, k, v, cache_k, cache_v, layer, page_table):
    t = q.shape[1]
    ppb = MOBA_BLOCK // PAGE_SIZE
    n_pages = page_table.shape[1]
    c_past = PAST_LEN // MOBA_BLOCK
    own_pages = n_pages - c_past * ppb
    n_sel = min(MOBA_TOP_K, c_past)
    heads_b = jnp.arange(N_HEADS)[:, None, None, None, None]
    rows = jnp.arange(PAGE_SIZE)
    own_mask = jnp.concatenate([jnp.ones((t, own_pages * PAGE_SIZE), bool), jnp.tril(jnp.ones((t, t), bool))], axis=1)

    def one(args):
        qb, kn, vn, pt = args
        qh = qb.transpose(1, 0, 2)
        own_pt = pt[c_past * ppb:]
        k_own = jnp.concatenate([cache_k[layer, own_pt].reshape(own_pages * PAGE_SIZE, N_HEADS, HEAD_DIM).astype(kn.dtype), kn], axis=0).transpose(1, 0, 2)
        v_own = jnp.concatenate([cache_v[layer, own_pt].reshape(own_pages * PAGE_SIZE, N_HEADS, HEAD_DIM).astype(vn.dtype), vn], axis=0).transpose(1, 0, 2)
        if n_sel > 0:
            past_pt = pt[:c_past * ppb]
            kpast = cache_k[layer, past_pt]
            kmean = jnp.mean(kpast.reshape(c_past, MOBA_BLOCK, N_HEADS, HEAD_DIM).astype(jnp.float32), axis=1)
            gate = jnp.einsum('htd,chd->htc', qh.astype(jnp.float32), kmean)
            _, sel = lax.top_k(gate, n_sel)
            phys = past_pt.reshape(c_past, ppb)[sel]
            k_sel = cache_k[layer, phys[..., None], rows, heads_b].reshape(N_HEADS, t, n_sel, MOBA_BLOCK, HEAD_DIM).astype(qb.dtype)
            v_sel = cache_v[layer, phys[..., None], rows, heads_b].reshape(N_HEADS, t, n_sel, MOBA_BLOCK, HEAD_DIM).astype(vn.dtype)
        else:
            k_sel = jnp.zeros((N_HEADS, t, 0, MOBA_BLOCK, HEAD_DIM), qb.dtype)
            v_sel = jnp.zeros((N_HEADS, t, 0, MOBA_BLOCK, HEAD_DIM), vn.dtype)
        valid = jnp.ones((t, n_sel), bool)
        return moba_core(qh, k_own, v_own, own_mask, k_sel, v_sel, valid)

    out = lax.map(one, (q, k, v, page_table))
    return out.transpose(0, 2, 1, 3)


def mlstm_project(xn, w_in, b_if):
    n, t, _ = xn.shape
    f32 = jnp.float32
    p = (xn @ w_in).astype(f32)
    dq = M_HEADS * M_QK_DIM
    dv = M_HEADS * M_V_DIM
    q = p[..., :dq]
    k = p[..., dq:2 * dq]
    v = p[..., 2 * dq:2 * dq + dv]
    o = p[..., 2 * dq + dv:2 * dq + 2 * dv]
    gi = p[..., 2 * dq + 2 * dv:2 * dq + 2 * dv + M_HEADS] + b_if[:M_HEADS].astype(f32)
    gf = p[..., 2 * dq + 2 * dv + M_HEADS:] + b_if[M_HEADS:].astype(f32)

    def heads(a, d):
        return a.reshape(n, t, M_HEADS, d).transpose(0, 2, 1, 3)

    q = heads(q, M_QK_DIM)
    k = heads(k, M_QK_DIM) * (M_QK_DIM ** -0.5)
    v = heads(v, M_V_DIM)
    logi = (IGATE_CAP * jnp.tanh(gi / IGATE_CAP)).transpose(0, 2, 1)
    logf = jax.nn.log_sigmoid(gf).transpose(0, 2, 1)
    return q, k, v, logi, logf, o


def mlstm_chunk(c0, n0, m0, q, k, v, logi, logf):
    L = q.shape[2]
    b = jnp.cumsum(logf, axis=-1)
    causal = jnp.tril(jnp.ones((L, L), bool))
    dlog = jnp.where(causal, b[..., :, None] - b[..., None, :] + logi[..., None, :], -jnp.inf)
    inter = b + m0[..., None]
    m = jnp.maximum(inter, jnp.max(dlog, axis=-1))
    w = jnp.exp(dlog - m[..., None])
    a = jnp.exp(inter - m)
    s = jnp.einsum('nhtd,nhsd->nhts', q, k) * w
    num = jnp.einsum('nhts,nhsv->nhtv', s, v) + a[..., None] * jnp.einsum('nhtd,nhdv->nhtv', q, c0)
    den = jnp.sum(s, axis=-1) + a * jnp.einsum('nhtd,nhd->nht', q, n0)
    h = num / jnp.maximum(jnp.abs(den), jnp.exp(-m))[..., None]
    m_end = m[..., -1]
    wk = jnp.exp(b[..., -1:] - b + logi - m_end[..., None])
    a_end = a[..., -1]
    c_new = a_end[..., None, None] * c0 + jnp.einsum('nhs,nhsd,nhsv->nhdv', wk, k, v)
    n_new = a_end[..., None] * n0 + jnp.einsum('nhs,nhsd->nhd', wk, k)
    return h, (c_new, n_new, m_end)


def mlstm_prompt(q, k, v, logi, logf):
    n, nh, s_, _ = q.shape
    f32 = jnp.float32
    nc = s_ // M_CHUNK

    def chunks(a):
        return jnp.moveaxis(a.reshape(n, nh, nc, M_CHUNK, *a.shape[3:]), 2, 0)

    init = (jnp.zeros((n, nh, M_QK_DIM, M_V_DIM), f32), jnp.zeros((n, nh, M_QK_DIM), f32), jnp.zeros((n, nh), f32))

    def step(carry, xs_):
        h, new = mlstm_chunk(*carry, *xs_)
        return new, h

    final, hs = lax.scan(step, init, (chunks(q), chunks(k), chunks(v), chunks(logi), chunks(logf)))
    h = jnp.moveaxis(hs, 0, 2).reshape(n, nh, s_, M_V_DIM)
    return h, final


def mlstm_out(h, o_pre, g_h, w_out, dtype):
    n, nh, t, dv = h.shape
    h = h.transpose(0, 2, 1, 3)
    h = h * lax.rsqrt(jnp.mean(h * h, axis=-1, keepdims=True) + EPS) * g_h.astype(jnp.float32).reshape(M_HEADS, M_V_DIM)
    h = h.reshape(n, t, nh * dv) * jax.nn.sigmoid(o_pre)
    return h.astype(dtype) @ w_out


def setup_inputs(seed: int = 0) -> dict:
    key = jax.random.key(seed)
    ks = jax.random.split(key, 24)
    f32 = jnp.float32
    n_pages = PAST_LEN // PAGE_SIZE
    pool = (5 * DEC_BATCH * n_pages + 3) // 4

    def nrm(k, shape, scale):
        return jax.random.normal(k, shape, f32) * scale

    def gain(k, shape):
        return 1.0 + nrm(k, shape, 0.05)

    d = D_MODEL
    d_att = N_HEADS * HEAD_DIM
    d_mv = M_HEADS * M_V_DIM
    m_in = 2 * M_HEADS * M_QK_DIM + 2 * d_mv + 2 * M_HEADS
    b_i = -2.0 + nrm(ks[9], (N_MLSTM_LAYERS, M_HEADS), 0.1)
    b_f = jnp.linspace(3.0, 6.0, M_HEADS, dtype=f32)[None, :] + nrm(ks[10], (N_MLSTM_LAYERS, M_HEADS), 0.1)
    page_table = jax.random.permutation(ks[7], pool)[:DEC_BATCH * n_pages].reshape(DEC_BATCH, n_pages).astype(jnp.int32)
    return {
        'x_prompt': nrm(ks[0], (BATCH, SEQ, d), 1.0),
        'x_sample': nrm(ks[1], (DEC_BATCH, DEC_SEQ, d), 1.0),
        'cache_k': nrm(ks[2], (N_ATTN_LAYERS, pool, PAGE_SIZE, N_HEADS, HEAD_DIM), 1.0),
        'cache_v': nrm(ks[3], (N_ATTN_LAYERS, pool, PAGE_SIZE, N_HEADS, HEAD_DIM), 1.0),
        'state_C': nrm(ks[4], (N_MLSTM_LAYERS, DEC_BATCH, M_HEADS, M_QK_DIM, M_V_DIM), 0.5),
        'state_n': nrm(ks[5], (N_MLSTM_LAYERS, DEC_BATCH, M_HEADS, M_QK_DIM), 0.5),
        'state_m': nrm(ks[6], (N_MLSTM_LAYERS, DEC_BATCH, M_HEADS), 2.0),
        'page_table': page_table,
        'g_ffn1': gain(ks[8], (DEPTH, d)),
        'w_ffn1_up': nrm(ks[11], (DEPTH, d, 2 * D_FF), d ** -0.5),
        'w_ffn1_down': nrm(ks[12], (DEPTH, D_FF, d), D_FF ** -0.5),
        'g_mix': gain(ks[13], (DEPTH, d)),
        'attn_w_qkv': nrm(ks[14], (N_ATTN_LAYERS, d, 3 * d_att), d ** -0.5),
        'attn_g_q': gain(ks[15], (N_ATTN_LAYERS, HEAD_DIM)),
        'attn_g_k': gain(ks[16], (N_ATTN_LAYERS, HEAD_DIM)),
        'attn_w_o': nrm(ks[17], (N_ATTN_LAYERS, d_att, d), d_att ** -0.5),
        'mlstm_w_in': nrm(ks[18], (N_MLSTM_LAYERS, d, m_in), d ** -0.5),
        'mlstm_b_if': jnp.concatenate([b_i, b_f], axis=-1),
        'mlstm_g_h': gain(ks[19], (N_MLSTM_LAYERS, d_mv)),
        'mlstm_w_out': nrm(ks[20], (N_MLSTM_LAYERS, d_mv, d), d_mv ** -0.5),
        'g_ffn2': gain(ks[21], (DEPTH, d)),
        'w_ffn2_up': nrm(ks[22], (DEPTH, d, 2 * D_FF), d ** -0.5),
        'w_ffn2_down': nrm(ks[23], (DEPTH, D_FF, d), D_FF ** -0.5),
    }


def reference(x_prompt, x_sample, cache_k, cache_v, state_C, state_n, state_m, page_table,
              g_ffn1, w_ffn1_up, w_ffn1_down, g_mix, attn_w_qkv, attn_g_q, attn_g_k, attn_w_o,
              mlstm_w_in, mlstm_b_if, mlstm_g_h, mlstm_w_out, g_ffn2, w_ffn2_up, w_ffn2_down):
    f32 = jnp.float32
    b_, s_ = x_prompt.shape[:2]
    db, t = x_sample.shape[:2]
    pos_p = jnp.arange(s_)
    pos_s = PAST_LEN + jnp.arange(t)
    xp, xs = x_prompt, x_sample
    kp_l, vp_l, ks_l, vs_l = [], [], [], []
    cp_l, np_l, mp_l, cs_l, ns_l, ms_l = [], [], [], [], [], []
    for i in range(DEPTH):
        xp = xp + 0.5 * swiglu(rms_norm(xp, g_ffn1[i]), w_ffn1_up[i], w_ffn1_down[i])
        xs = xs + 0.5 * swiglu(rms_norm(xs, g_ffn1[i]), w_ffn1_up[i], w_ffn1_down[i])
        hp = rms_norm(xp, g_mix[i])
        hs = rms_norm(xs, g_mix[i])
        j = i // N_MIXERS
        if i % N_MIXERS == 0:
            qp, kp, vp = attn_qkv(hp, attn_w_qkv[j], attn_g_q[j], attn_g_k[j], pos_p)
            op = moba_prompt(qp, kp, vp)
            qs, ks, vs = attn_qkv(hs, attn_w_qkv[j], attn_g_q[j], attn_g_k[j], pos_s)
            os_ = moba_sample(qs, ks, vs, cache_k, cache_v, j, page_table)
            xp = xp + op.reshape(b_, s_, N_HEADS * HEAD_DIM) @ attn_w_o[j]
            xs = xs + os_.reshape(db, t, N_HEADS * HEAD_DIM) @ attn_w_o[j]
            kp_l.append(kp.reshape(b_, s_ // PAGE_SIZE, PAGE_SIZE, N_HEADS, HEAD_DIM))
            vp_l.append(vp.reshape(b_, s_ // PAGE_SIZE, PAGE_SIZE, N_HEADS, HEAD_DIM))
            ks_l.append(ks)
            vs_l.append(vs)
        else:
            q, k, v, li, lf, o = mlstm_project(hp, mlstm_w_in[j], mlstm_b_if[j])
            h, (c_p, n_p, m_p) = mlstm_prompt(q, k, v, li, lf)
            xp = xp + mlstm_out(h, o, mlstm_g_h[j], mlstm_w_out[j], xp.dtype)
            q, k, v, li, lf, o = mlstm_project(hs, mlstm_w_in[j], mlstm_b_if[j])
            h, (c_s, n_s, m_s) = mlstm_chunk(state_C[j].astype(f32), state_n[j].astype(f32), state_m[j].astype(f32), q, k, v, li, lf)
            xs = xs + mlstm_out(h, o, mlstm_g_h[j], mlstm_w_out[j], xs.dtype)
            cp_l.append(c_p)
            np_l.append(n_p)
            mp_l.append(m_p)
            cs_l.append(c_s)
            ns_l.append(n_s)
            ms_l.append(m_s)
        xp = xp + 0.5 * swiglu(rms_norm(xp, g_ffn2[i]), w_ffn2_up[i], w_ffn2_down[i])
        xs = xs + 0.5 * swiglu(rms_norm(xs, g_ffn2[i]), w_ffn2_up[i], w_ffn2_down[i])
    return (xp, xs, jnp.stack(kp_l), jnp.stack(vp_l), jnp.stack(ks_l), jnp.stack(vs_l),
            jnp.stack(cp_l), jnp.stack(np_l), jnp.stack(mp_l), jnp.stack(cs_l), jnp.stack(ns_l), jnp.stack(ms_l))
```

```python
import functools

import jax
import jax.numpy as jnp
from jax import lax
from jax.experimental import pallas as pl
from jax.experimental.pallas import tpu as pltpu

F32 = jnp.float32
BF16 = jnp.bfloat16
HIGHEST = lax.Precision.HIGHEST

D_MODEL = 2048
DEPTH = 4
PAST_LEN = 16384
PAGE_SIZE = 128
N_HEADS = 16
HEAD_DIM = 128
MOBA_BLOCK = 256
MOBA_TOP_K = 3
ROPE_THETA = 10000.0
M_HEADS = 8
M_QK_DIM = 128
M_V_DIM = 256
M_CHUNK = 64
IGATE_CAP = 15.0
D_FF = 5632
EPS = 1e-6

D_MQ = M_HEADS * M_QK_DIM
D_MV = M_HEADS * M_V_DIM
M_MAIN = 2 * D_MQ + 2 * D_MV
SAMPLE_T_PAD = 8
PAGES_PER_BLOCK = MOBA_BLOCK // PAGE_SIZE

TM_PROMPT = 512
TF_FFN = 512
TN_PROJ = 512
VMEM_LIMIT = 56 * 1024 * 1024


def _cparams(sem):
    return pltpu.CompilerParams(dimension_semantics=sem, vmem_limit_bytes=VMEM_LIMIT)


def _dot(a, b):
    return jnp.dot(a, b, preferred_element_type=F32)


def _dot_nt(a, b, precision=None):
    return lax.dot_general(a, b, (((1,), (1,)), ((), ())), precision=precision,
                           preferred_element_type=F32)


def _dot_tn(a, b):
    return lax.dot_general(a, b, (((0,), (0,)), ((), ())), preferred_element_type=F32)


def _rms_norm_rows(x, g):
    ms = jnp.mean(x * x, axis=-1, keepdims=True)
    return x * lax.rsqrt(ms + EPS) * g


def _ffn_kernel(x_ref, g_ref, wg_ref, wu_ref, wd_ref, o_ref, xn_ref, acc_ref):
    j = pl.program_id(1)

    @pl.when(j == 0)
    def _():
        xn_ref[...] = _rms_norm_rows(x_ref[...], g_ref[...]).astype(BF16)
        acc_ref[...] = jnp.zeros_like(acc_ref)

    xn = xn_ref[...]
    gate = _dot(xn, wg_ref[...])
    up = _dot(xn, wu_ref[...])
    h = (gate * jax.nn.sigmoid(gate) * up).astype(BF16)
    acc_ref[...] += _dot(h, wd_ref[...])

    @pl.when(j == pl.num_programs(1) - 1)
    def _():
        o_ref[...] = x_ref[...] + 0.5 * acc_ref[...]


def _ffn(x, g, w_up, w_down, layer, tm):
    t = x.shape[0]
    nj = D_FF // TF_FFN
    return pl.pallas_call(
        _ffn_kernel,
        grid=(t // tm, nj),
        in_specs=[
            pl.BlockSpec((tm, D_MODEL), lambda i, j: (i, 0)),
            pl.BlockSpec((None, 1, D_MODEL), lambda i, j: (layer, 0, 0)),
            pl.BlockSpec((None, D_MODEL, TF_FFN), lambda i, j: (layer, 0, j)),
            pl.BlockSpec((None, D_MODEL, TF_FFN), lambda i, j: (layer, 0, j + nj)),
            pl.BlockSpec((None, TF_FFN, D_MODEL), lambda i, j: (layer, j, 0)),
        ],
        out_specs=pl.BlockSpec((tm, D_MODEL), lambda i, j: (i, 0)),
        out_shape=jax.ShapeDtypeStruct((t, D_MODEL), F32),
        scratch_shapes=[pltpu.VMEM((tm, D_MODEL), BF16), pltpu.VMEM((tm, D_MODEL), F32)],
        compiler_params=_cparams(("parallel", "arbitrary")),
        name="ffn",
    )(x, g, w_up, w_up, w_down)


def _head_norm_rope(y, g, cos, sin):
    outs = []
    for hh in range(TN_PROJ // HEAD_DIM):
        yn = _rms_norm_rows(y[:, hh * HEAD_DIM:(hh + 1) * HEAD_DIM], g)
        outs.append(yn * cos + pltpu.roll(yn, HEAD_DIM // 2, 1) * sin)
    return jnp.concatenate(outs, axis=1)


def _qkv_kernel(x_ref, g_ref, w_ref, gq_ref, gk_ref, cos_ref, sin_ref, q_ref, k_ref, v_ref, xn_ref):
    j = pl.program_id(1)
    n_q = D_MODEL // TN_PROJ

    @pl.when(j == 0)
    def _():
        xn_ref[...] = _rms_norm_rows(x_ref[...], g_ref[...]).astype(BF16)

    y = _dot(xn_ref[...], w_ref[...])

    @pl.when(j < n_q)
    def _():
        q_ref[...] = _head_norm_rope(y, gq_ref[...], cos_ref[...], sin_ref[...])

    @pl.when((j >= n_q) & (j < 2 * n_q))
    def _():
        k_ref[...] = _head_norm_rope(y, gk_ref[...], cos_ref[...], sin_ref[...])

    @pl.when(j >= 2 * n_q)
    def _():
        v_ref[...] = y


def _qkv(x, g_mix, w_qkv, g_q, g_k, cos, sin, layer, j_attn, tm):
    t = x.shape[0]
    n_q = D_MODEL // TN_PROJ
    n_pos_blocks = cos.shape[0] // tm
    out_sd = jax.ShapeDtypeStruct((t, D_MODEL), F32)
    return pl.pallas_call(
        _qkv_kernel,
        grid=(t // tm, 3 * n_q),
        in_specs=[
            pl.BlockSpec((tm, D_MODEL), lambda i, j: (i, 0)),
            pl.BlockSpec((None, 1, D_MODEL), lambda i, j: (layer, 0, 0)),
            pl.BlockSpec((None, D_MODEL, TN_PROJ), lambda i, j: (j_attn, 0, j)),
            pl.BlockSpec((None, 1, HEAD_DIM), lambda i, j: (j_attn, 0, 0)),
            pl.BlockSpec((None, 1, HEAD_DIM), lambda i, j: (j_attn, 0, 0)),
            pl.BlockSpec((tm, HEAD_DIM), lambda i, j: (i % n_pos_blocks, 0)),
            pl.BlockSpec((tm, HEAD_DIM), lambda i, j: (i % n_pos_blocks, 0)),
        ],
        out_specs=[
            pl.BlockSpec((tm, TN_PROJ), lambda i, j: (i, jnp.minimum(j, n_q - 1))),
            pl.BlockSpec((tm, TN_PROJ), lambda i, j: (i, jnp.clip(j - n_q, 0, n_q - 1))),
            pl.BlockSpec((tm, TN_PROJ), lambda i, j: (i, jnp.clip(j - 2 * n_q, 0, n_q - 1))),
        ],
        out_shape=[out_sd, out_sd, out_sd],
        scratch_shapes=[pltpu.VMEM((tm, D_MODEL), BF16)],
        compiler_params=_cparams(("parallel", "arbitrary")),
        name="qkv",
    )(x, g_mix, w_qkv, g_q, g_k, cos, sin)


def _moba_prompt_kernel(q_ref, k_ref, v_ref, o_ref, kb_ref, vb_ref, kmean_ref):
    i = pl.program_id(2)
    seq = k_ref.shape[0]
    nb = seq // MOBA_BLOCK
    bq = MOBA_BLOCK

    @pl.when(i == 0)
    def _():
        k = k_ref[...]
        kb_ref[...] = k.astype(BF16)
        vb_ref[...] = v_ref[...].astype(BF16)
        kmean_ref[...] = jnp.zeros_like(kmean_ref)
        kmean_ref[0:nb, :] = jnp.mean(k.reshape(nb, MOBA_BLOCK, HEAD_DIM), axis=1)

    q = q_ref[...]
    gate = _dot_nt(q, kmean_ref[...], precision=HIGHEST)
    lane = lax.broadcasted_iota(jnp.int32, gate.shape, 1)
    cnt = jnp.zeros(gate.shape, jnp.int32)
    for kk in range(nb):
        gk = gate[:, kk:kk + 1]
        beats = (gk > gate) | ((gk == gate) & (kk < lane))
        cnt = cnt + jnp.where(beats & (kk < i), 1, 0)
    sel = jnp.where((cnt < MOBA_TOP_K) & (lane < i), 1, 0)

    s = _dot_nt(q.astype(BF16), kb_ref[...]) * (HEAD_DIM ** -0.5)
    row = lax.broadcasted_iota(jnp.int32, (bq, MOBA_BLOCK), 0)
    col = lax.broadcasted_iota(jnp.int32, (bq, MOBA_BLOCK), 1)
    causal = col <= row
    pieces = []
    for j in range(nb):
        allow = (sel[:, j:j + 1] > 0) | ((i == j) & causal)
        pieces.append(jnp.where(allow, s[:, j * MOBA_BLOCK:(j + 1) * MOBA_BLOCK], -jnp.inf))
    s = jnp.concatenate(pieces, axis=1)
    m = jnp.max(s, axis=-1, keepdims=True)
    p = jnp.exp(s - m)
    l = jnp.sum(p, axis=-1, keepdims=True)
    o_ref[...] = (_dot(p.astype(BF16), vb_ref[...]) / l).astype(o_ref.dtype)


def _moba_prompt(q, k, v, batch, seq):
    nqb = seq // MOBA_BLOCK
    return pl.pallas_call(
        _moba_prompt_kernel,
        grid=(batch, N_HEADS, nqb),
        in_specs=[
            pl.BlockSpec((MOBA_BLOCK, HEAD_DIM), lambda b, h, i: (b * nqb + i, h)),
            pl.BlockSpec((seq, HEAD_DIM), lambda b, h, i: (b, h)),
            pl.BlockSpec((seq, HEAD_DIM), lambda b, h, i: (b, h)),
        ],
        out_specs=pl.BlockSpec((MOBA_BLOCK, HEAD_DIM), lambda b, h, i: (b * nqb + i, h)),
        out_shape=jax.ShapeDtypeStruct((batch * seq, N_HEADS * HEAD_DIM), BF16),
        scratch_shapes=[
            pltpu.VMEM((seq, HEAD_DIM), BF16),
            pltpu.VMEM((seq, HEAD_DIM), BF16),
            pltpu.VMEM((HEAD_DIM, HEAD_DIM), F32),
        ],
        compiler_params=_cparams(("parallel", "parallel", "arbitrary")),
        name="moba_prompt",
    )(q, k, v)


def _proj_res_kernel(a_ref, w_ref, x_ref, o_ref):
    o_ref[...] = x_ref[...] + _dot(a_ref[...].astype(BF16), w_ref[...])


def _proj_res(a, w, x, j_w, tm):
    t = x.shape[0]
    return pl.pallas_call(
        _proj_res_kernel,
        grid=(t // tm,),
        in_specs=[
            pl.BlockSpec((tm, D_MODEL), lambda i: (i, 0)),
            pl.BlockSpec((None, D_MODEL, D_MODEL), lambda i: (j_w, 0, 0)),
            pl.BlockSpec((tm, D_MODEL), lambda i: (i, 0)),
        ],
        out_specs=pl.BlockSpec((tm, D_MODEL), lambda i: (i, 0)),
        out_shape=jax.ShapeDtypeStruct((t, D_MODEL), F32),
        compiler_params=_cparams(("parallel",)),
        name="proj_res",
    )(a, w, x)


def _mlstm_in_kernel(x_ref, g_ref, w_ref, wg_ref, y_ref, gates_ref, xn_ref):
    j = pl.program_id(1)

    @pl.when(j == 0)
    def _():
        xn = _rms_norm_rows(x_ref[...], g_ref[...]).astype(BF16)
        xn_ref[...] = xn
        gates_ref[...] = _dot(xn, wg_ref[...])

    y_ref[...] = _dot(xn_ref[...], w_ref[...])


def _mlstm_in(x, g_mix, w_main, w_gates, layer, j_m, tm):
    t = x.shape[0]
    return pl.pallas_call(
        _mlstm_in_kernel,
        grid=(t // tm, M_MAIN // TN_PROJ),
        in_specs=[
            pl.BlockSpec((tm, D_MODEL), lambda i, j: (i, 0)),
            pl.BlockSpec((None, 1, D_MODEL), lambda i, j: (layer, 0, 0)),
            pl.BlockSpec((None, D_MODEL, TN_PROJ), lambda i, j: (j_m, 0, j)),
            pl.BlockSpec((None, D_MODEL, 128), lambda i, j: (j_m, 0, 0)),
        ],
        out_specs=[
            pl.BlockSpec((tm, TN_PROJ), lambda i, j: (i, j)),
            pl.BlockSpec((tm, 128), lambda i, j: (i, 0)),
        ],
        out_shape=[jax.ShapeDtypeStruct((t, M_MAIN), F32), jax.ShapeDtypeStruct((t, 128), F32)],
        scratch_shapes=[pltpu.VMEM((tm, D_MODEL), BF16)],
        compiler_params=_cparams(("parallel", "arbitrary")),
        name="mlstm_in",
    )(x, g_mix, w_main, w_gates)


def _mlstm_chunk_kernel(l_valid, zero_init, *refs):
    if zero_init:
        main_ref, gates_ref, bias_ref, gh_ref = refs[:4]
        rest = refs[4:]
    else:
        main_ref, gates_ref, bias_ref, gh_ref, c0_ref, n0_ref, m0_ref = refs[:7]
        rest = refs[7:]
    hn_ref, cout_ref, nout_ref, mout_ref, c_s, n_s, m_s = rest
    c = pl.program_id(1)
    L = main_ref.shape[0]
    scale_k = M_QK_DIM ** -0.5

    @pl.when(c == 0)
    def _():
        if zero_init:
            c_s[...] = jnp.zeros_like(c_s)
            n_s[...] = jnp.zeros_like(n_s)
            m_s[...] = jnp.zeros_like(m_s)
        else:
            c_s[...] = c0_ref[...]
            n_s[...] = n0_ref[...]
            for h in range(M_HEADS):
                m_s[h] = jnp.broadcast_to(m0_ref[h], (8, 128))

    pre = gates_ref[...] + bias_ref[...]
    lane = lax.broadcasted_iota(jnp.int32, pre.shape, 1)
    act = jnp.where(lane < M_HEADS, IGATE_CAP * jnp.tanh(pre / IGATE_CAP), jax.nn.log_sigmoid(pre))
    r = lax.broadcasted_iota(jnp.int32, (L, L), 0)
    cc = lax.broadcasted_iota(jnp.int32, (L, L), 1)
    causal = cc <= r
    bcols = jnp.dot(causal.astype(F32), act, precision=HIGHEST, preferred_element_type=F32)
    act_t = act.T
    b_t = bcols.T
    rows = lax.broadcasted_iota(jnp.int32, (L, 1), 0)

    for h in range(M_HEADS):
        q = main_ref[:, h * M_QK_DIM:(h + 1) * M_QK_DIM]
        k = main_ref[:, D_MQ + h * M_QK_DIM:D_MQ + (h + 1) * M_QK_DIM]
        v = main_ref[:, 2 * D_MQ + h * M_V_DIM:2 * D_MQ + (h + 1) * M_V_DIM]
        o = main_ref[:, 2 * D_MQ + D_MV + h * M_V_DIM:2 * D_MQ + D_MV + (h + 1) * M_V_DIM]
        b_col = bcols[:, M_HEADS + h:M_HEADS + h + 1]
        b_row = b_t[M_HEADS + h:M_HEADS + h + 1, :]
        li_col = act[:, h:h + 1]
        li_row = act_t[h:h + 1, :]
        c0 = c_s[h]
        n0 = n_s[h]
        m0 = m_s[h][0:1, 0:1]

        dlog = jnp.where(causal, b_col - b_row + li_row, -jnp.inf)
        inter = b_col + m0
        m = jnp.maximum(inter, jnp.max(dlog, axis=-1, keepdims=True))
        w = jnp.exp(dlog - m)
        a = jnp.exp(inter - m)
        qb = q.astype(BF16)
        kb = k.astype(BF16)
        vb = v.astype(BF16)
        s = _dot_nt(qb, kb) * scale_k * w
        num = _dot(s.astype(BF16), vb) + a * _dot(qb, c0.astype(BF16))
        den = jnp.sum(s, axis=-1, keepdims=True) + a * jnp.sum(q * n0, axis=-1, keepdims=True)
        hh = num / jnp.maximum(jnp.abs(den), jnp.exp(-m))
        hn = hh * lax.rsqrt(jnp.mean(hh * hh, axis=-1, keepdims=True) + EPS)
        hn = hn * gh_ref[:, h * M_V_DIM:(h + 1) * M_V_DIM] * jax.nn.sigmoid(o)
        hn_ref[:, h * M_V_DIM:(h + 1) * M_V_DIM] = hn.astype(hn_ref.dtype)

        m_end = m[l_valid - 1:l_valid]
        a_end = a[l_valid - 1:l_valid]
        b_last = b_col[l_valid - 1:l_valid]
        wk = jnp.exp(b_last - b_col + li_col - m_end)
        if l_valid < L:
            wk = jnp.where(rows < l_valid, wk, 0.0)
        kw = k * (scale_k * wk)
        c_s[h] = a_end * c0 + _dot_tn(kw.astype(BF16), vb)
        n_s[h] = a_end * n0 + jnp.sum(kw, axis=0, keepdims=True)
        m_s[h] = jnp.broadcast_to(m_end, (8, 128))

    @pl.when(c == pl.num_programs(1) - 1)
    def _():
        cout_ref[...] = c_s[...]
        nout_ref[...] = n_s[...]
        mout_ref[...] = m_s[...]


def _mlstm_chunk(main, gates, bias, g_h, j_m, n_seq, n_chunks, chunk, l_valid, state=None):
    zero_init = state is None
    t = main.shape[0]
    in_specs = [
        pl.BlockSpec((chunk, M_MAIN), lambda n, c: (n * n_chunks + c, 0)),
        pl.BlockSpec((chunk, 128), lambda n, c: (n * n_chunks + c, 0)),
        pl.BlockSpec((None, 1, 128), lambda n, c: (j_m, 0, 0)),
        pl.BlockSpec((None, 1, D_MV), lambda n, c: (j_m, 0, 0)),
    ]
    args = [main, gates, bias, g_h]
    if not zero_init:
        in_specs += [
            pl.BlockSpec((None, None, M_HEADS, M_QK_DIM, M_V_DIM), lambda n, c: (j_m, n, 0, 0, 0)),
            pl.BlockSpec((None, None, M_HEADS, 1, M_QK_DIM), lambda n, c: (j_m, n, 0, 0, 0)),
            pl.BlockSpec((None, None, M_HEADS, 1, 1), lambda n, c: (j_m, n, 0, 0, 0)),
        ]
        args += list(state)
    hn_dtype = BF16 if chunk % 16 == 0 else F32
    return pl.pallas_call(
        functools.partial(_mlstm_chunk_kernel, l_valid, zero_init),
        grid=(n_seq, n_chunks),
        in_specs=in_specs,
        out_specs=[
            pl.BlockSpec((chunk, D_MV), lambda n, c: (n * n_chunks + c, 0)),
            pl.BlockSpec((None, M_HEADS, M_QK_DIM, M_V_DIM), lambda n, c: (n, 0, 0, 0)),
            pl.BlockSpec((None, M_HEADS, 1, M_QK_DIM), lambda n, c: (n, 0, 0, 0)),
            pl.BlockSpec((None, M_HEADS, 8, 128), lambda n, c: (n, 0, 0, 0)),
        ],
        out_shape=[
            jax.ShapeDtypeStruct((t, D_MV), hn_dtype),
            jax.ShapeDtypeStruct((n_seq, M_HEADS, M_QK_DIM, M_V_DIM), F32),
            jax.ShapeDtypeStruct((n_seq, M_HEADS, 1, M_QK_DIM), F32),
            jax.ShapeDtypeStruct((n_seq, M_HEADS, 8, 128), F32),
        ],
        scratch_shapes=[
            pltpu.VMEM((M_HEADS, M_QK_DIM, M_V_DIM), F32),
            pltpu.VMEM((M_HEADS, 1, M_QK_DIM), F32),
            pltpu.VMEM((M_HEADS, 8, 128), F32),
        ],
        compiler_params=_cparams(("parallel", "arbitrary")),
        name="mlstm_chunk",
    )(*args)


_BLOCKS_PER_STEP = 4
_PAGES_PER_STEP = _BLOCKS_PER_STEP * PAGES_PER_BLOCK


def _sample_select_kernel(pt_ref, q_ref, *refs):
    page_refs = refs[:_PAGES_PER_STEP]
    sel_ref, kmean_ref = refs[_PAGES_PER_STEP:]
    s = pl.program_id(1)
    c_past = kmean_ref.shape[0]

    for u in range(_BLOCKS_PER_STEP):
        tot = jnp.zeros((1, N_HEADS * HEAD_DIM), F32)
        for pp in range(PAGES_PER_BLOCK):
            tot = tot + jnp.sum(page_refs[u * PAGES_PER_BLOCK + pp][...], axis=0, keepdims=True)
        kmean_ref[pl.ds(s * _BLOCKS_PER_STEP + u, 1), :] = tot * (1.0 / MOBA_BLOCK)

    @pl.when(s == pl.num_programs(1) - 1)
    def _():
        out_lane = lax.broadcasted_iota(jnp.int32, sel_ref.shape, 1)
        out = jnp.zeros(sel_ref.shape, jnp.int32)
        for h in range(N_HEADS):
            g = _dot_nt(q_ref[:, h * HEAD_DIM:(h + 1) * HEAD_DIM],
                        kmean_ref[:, h * HEAD_DIM:(h + 1) * HEAD_DIM], precision=HIGHEST)
            lane = lax.broadcasted_iota(jnp.int32, g.shape, 1)
            for kk in range(MOBA_TOP_K):
                mx = jnp.max(g, axis=-1, keepdims=True)
                idx = jnp.min(jnp.where(g == mx, lane, c_past), axis=-1, keepdims=True)
                out = jnp.where(out_lane == h * MOBA_TOP_K + kk, idx, out)
                g = jnp.where(lane == idx, -jnp.inf, g)
        sel_ref[...] = out


def _sample_select(page_table, q, cache_k, j_attn, n_db):
    n_pages = page_table.shape[1]
    c_past = n_pages // PAGES_PER_BLOCK
    n_steps = c_past // _BLOCKS_PER_STEP

    def page_spec(u):
        return pl.BlockSpec((None, None, PAGE_SIZE, N_HEADS * HEAD_DIM),
                            lambda db, s, pt: (j_attn, pt[db, s * _PAGES_PER_STEP + u], 0, 0))

    grid_spec = pltpu.PrefetchScalarGridSpec(
        num_scalar_prefetch=1,
        grid=(n_db, n_steps),
        in_specs=[pl.BlockSpec((SAMPLE_T_PAD, N_HEADS * HEAD_DIM), lambda db, s, pt: (db, 0))]
        + [page_spec(u) for u in range(_PAGES_PER_STEP)],
        out_specs=pl.BlockSpec((None, SAMPLE_T_PAD, 128), lambda db, s, pt: (db, 0, 0)),
        scratch_shapes=[pltpu.VMEM((c_past, N_HEADS * HEAD_DIM), F32)],
    )
    return pl.pallas_call(
        _sample_select_kernel,
        grid_spec=grid_spec,
        out_shape=jax.ShapeDtypeStruct((n_db, SAMPLE_T_PAD, 128), jnp.int32),
        compiler_params=_cparams(("parallel", "arbitrary")),
        name="sample_select",
    )(page_table, q, *([cache_k] * _PAGES_PER_STEP))


_N_SEL_PAGES = MOBA_TOP_K * PAGES_PER_BLOCK


def _sample_attn_kernel(t_valid, pt_ref, sel_ref, q_ref, kn_ref, vn_ref, *refs):
    k_refs = refs[:_N_SEL_PAGES]
    v_refs = refs[_N_SEL_PAGES:2 * _N_SEL_PAGES]
    o_ref = refs[2 * _N_SEL_PAGES]
    t = pl.program_id(2)
    scale = HEAD_DIM ** -0.5

    @pl.when(t == 0)
    def _():
        o_ref[...] = jnp.zeros_like(o_ref)

    q = jnp.broadcast_to(q_ref[pl.ds(t, 1), :], (SAMPLE_T_PAD, HEAD_DIM)).astype(BF16)
    ks = jnp.concatenate([r[...] for r in k_refs], axis=0).astype(BF16)
    vs = jnp.concatenate([r[...] for r in v_refs], axis=0).astype(BF16)
    s_sel = _dot_nt(q, ks) * scale
    s_own = _dot_nt(q, kn_ref[...].astype(BF16)) * scale
    lane = lax.broadcasted_iota(jnp.int32, s_own.shape, 1)
    s_own = jnp.where((lane <= t) & (lane < t_valid), s_own, -jnp.inf)
    m = jnp.maximum(jnp.max(s_sel, axis=-1, keepdims=True), jnp.max(s_own, axis=-1, keepdims=True))
    p_sel = jnp.exp(s_sel - m)
    p_own = jnp.exp(s_own - m)
    l = jnp.sum(p_sel, axis=-1, keepdims=True) + jnp.sum(p_own, axis=-1, keepdims=True)
    out = (_dot(p_sel.astype(BF16), vs) + _dot(p_own.astype(BF16), vn_ref[...].astype(BF16))) / l
    o_ref[pl.ds(t, 1), :] = out[0:1, :]


def _sample_attn(page_table, sel_flat, q, k_new, v_new, cache_k, cache_v, j_attn, n_db, t_valid):
    def page_spec(kk, pp):
        def index_map(db, h, t, pt, sel):
            blk = sel[db * (SAMPLE_T_PAD * 128) + t * 128 + h * MOBA_TOP_K + kk]
            return (j_attn, pt[db, blk * PAGES_PER_BLOCK + pp], 0, h)
        return pl.BlockSpec((None, None, PAGE_SIZE, HEAD_DIM), index_map)

    row_spec = pl.BlockSpec((SAMPLE_T_PAD, HEAD_DIM), lambda db, h, t, pt, sel: (db, h))
    page_specs = [page_spec(kk, pp) for kk in range(MOBA_TOP_K) for pp in range(PAGES_PER_BLOCK)]
    grid_spec = pltpu.PrefetchScalarGridSpec(
        num_scalar_prefetch=2,
        grid=(n_db, N_HEADS, t_valid),
        in_specs=[row_spec, row_spec, row_spec] + page_specs + page_specs,
        out_specs=row_spec,
    )
    return pl.pallas_call(
        functools.partial(_sample_attn_kernel, t_valid),
        grid_spec=grid_spec,
        out_shape=jax.ShapeDtypeStruct((n_db * SAMPLE_T_PAD, N_HEADS * HEAD_DIM), F32),
        compiler_params=_cparams(("parallel", "parallel", "arbitrary")),
        name="sample_attn",
    )(page_table, sel_flat, q, k_new, v_new, *([cache_k] * _N_SEL_PAGES), *([cache_v] * _N_SEL_PAGES))


def _rope_tables(pos):
    half = HEAD_DIM // 2
    inv = ROPE_THETA ** (-jnp.arange(half, dtype=F32) / half)
    ang = pos.astype(F32)[:, None] * inv[None, :]
    cos = jnp.cos(ang)
    sin = jnp.sin(ang)
    return jnp.concatenate([cos, cos], axis=-1), jnp.concatenate([-sin, sin], axis=-1)


def kernel(x_prompt, x_sample, cache_k, cache_v, state_C, state_n, state_m, page_table, g_ffn1, w_ffn1_up, w_ffn1_down, g_mix, attn_w_qkv, attn_g_q, attn_g_k, attn_w_o, mlstm_w_in, mlstm_b_if, mlstm_g_h, mlstm_w_out, g_ffn2, w_ffn2_up, w_ffn2_down):
    b_, s_, d = x_prompt.shape
    db, t_s, _ = x_sample.shape
    assert d == D_MODEL and s_ % MOBA_BLOCK == 0 and s_ % TM_PROMPT == 0 and t_s <= SAMPLE_T_PAD
    assert PAST_LEN % MOBA_BLOCK == 0 and page_table.shape[1] * PAGE_SIZE == PAST_LEN
    n_mlstm = mlstm_w_in.shape[0]
    pool = cache_k.shape[1]

    w1u, w1d = w_ffn1_up.astype(BF16), w_ffn1_down.astype(BF16)
    w2u, w2d = w_ffn2_up.astype(BF16), w_ffn2_down.astype(BF16)
    wqkv, wo = attn_w_qkv.astype(BF16), attn_w_o.astype(BF16)
    w_in_main = mlstm_w_in[:, :, :M_MAIN].astype(BF16)
    w_in_gates = jnp.pad(mlstm_w_in[:, :, M_MAIN:], ((0, 0), (0, 0), (0, 128 - 2 * M_HEADS))).astype(BF16)
    w_out = mlstm_w_out.astype(BF16)
    g1 = g_ffn1.reshape(DEPTH, 1, d)
    g2 = g_ffn2.reshape(DEPTH, 1, d)
    gm = g_mix.reshape(DEPTH, 1, d)
    gq = attn_g_q.reshape(-1, 1, HEAD_DIM)
    gk = attn_g_k.reshape(-1, 1, HEAD_DIM)
    b_if = jnp.pad(mlstm_b_if, ((0, 0), (0, 128 - 2 * M_HEADS))).reshape(n_mlstm, 1, 128)
    g_h = mlstm_g_h.reshape(n_mlstm, 1, D_MV)

    cos_p, sin_p = _rope_tables(jnp.arange(s_))
    cos_s, sin_s = _rope_tables(PAST_LEN + (jnp.arange(db * SAMPLE_T_PAD) % SAMPLE_T_PAD))

    cache_k2 = cache_k.reshape(cache_k.shape[0], pool, PAGE_SIZE, N_HEADS * HEAD_DIM)
    cache_v2 = cache_v.reshape(cache_v.shape[0], pool, PAGE_SIZE, N_HEADS * HEAD_DIM)
    st_n = state_n.reshape(n_mlstm, db, M_HEADS, 1, M_QK_DIM)
    st_m = state_m.reshape(n_mlstm, db, M_HEADS, 1, 1)

    xp = x_prompt.reshape(b_ * s_, d)
    xs = jnp.pad(x_sample, ((0, 0), (0, SAMPLE_T_PAD - t_s), (0, 0))).reshape(db * SAMPLE_T_PAD, d)
    tm_s = db * SAMPLE_T_PAD

    kp_l, vp_l, ks_l, vs_l = [], [], [], []
    cp_l, np_l, mp_l, cs_l, ns_l, ms_l = [], [], [], [], [], []
    for i in range(DEPTH):
        xp = _ffn(xp, g1, w1u, w1d, i, TM_PROMPT)
        xs = _ffn(xs, g1, w1u, w1d, i, tm_s)
        j = i // 2
        if i % 2 == 0:
            qp, kp, vp = _qkv(xp, gm, wqkv, gq, gk, cos_p, sin_p, i, j, TM_PROMPT)
            op = _moba_prompt(qp, kp, vp, b_, s_)
            xp = _proj_res(op, wo, xp, j, TM_PROMPT)
            qs, ks, vs = _qkv(xs, gm, wqkv, gq, gk, cos_s, sin_s, i, j, tm_s)
            sel = _sample_select(page_table, qs, cache_k2, j, db)
            os_ = _sample_attn(page_table, sel.reshape(-1), qs, ks, vs, cache_k2, cache_v2, j, db, t_s)
            xs = _proj_res(os_, wo, xs, j, tm_s)
            kp_l.append(kp.reshape(b_, s_ // PAGE_SIZE, PAGE_SIZE, N_HEADS, HEAD_DIM))
            vp_l.append(vp.reshape(b_, s_ // PAGE_SIZE, PAGE_SIZE, N_HEADS, HEAD_DIM))
            ks_l.append(ks.reshape(db, SAMPLE_T_PAD, N_HEADS, HEAD_DIM)[:, :t_s])
            vs_l.append(vs.reshape(db, SAMPLE_T_PAD, N_HEADS, HEAD_DIM)[:, :t_s])
        else:
            main, gates = _mlstm_in(xp, gm, w_in_main, w_in_gates, i, j, TM_PROMPT)
            hn, c_p, n_p, m_p = _mlstm_chunk(main, gates, b_if, g_h, j, b_, s_ // M_CHUNK, M_CHUNK, M_CHUNK)
            xp = _proj_res(hn, w_out, xp, j, TM_PROMPT)
            main, gates = _mlstm_in(xs, gm, w_in_main, w_in_gates, i, j, tm_s)
            hn, c_s, n_s, m_s = _mlstm_chunk(main, gates, b_if, g_h, j, db, 1, SAMPLE_T_PAD, t_s,
                                             state=(state_C, st_n, st_m))
            xs = _proj_res(hn, w_out, xs, j, tm_s)
            cp_l.append(c_p)
            np_l.append(n_p[:, :, 0, :])
            mp_l.append(m_p[:, :, 0, 0])
            cs_l.append(c_s)
            ns_l.append(n_s[:, :, 0, :])
            ms_l.append(m_s[:, :, 0, 0])
        xp = _ffn(xp, g2, w2u, w2d, i, TM_PROMPT)
        xs = _ffn(xs, g2, w2u, w2d, i, tm_s)

    y_prompt = xp.reshape(b_, s_, d)
    y_sample = xs.reshape(db, SAMPLE_T_PAD, d)[:, :t_s]
    return (y_prompt, y_sample, jnp.stack(kp_l), jnp.stack(vp_l), jnp.stack(ks_l), jnp.stack(vs_l),
            jnp.stack(cp_l), jnp.stack(np_l), jnp.stack(mp_l), jnp.stack(cs_l), jnp.stack(ns_l), jnp.stack(ms_l))
```

```python
import functools

import jax
import jax.numpy as jnp
from jax import lax
from jax.experimental import pallas as pl
from jax.experimental.pallas import tpu as pltpu

F32 = jnp.float32
BF16 = jnp.bfloat16
HIGHEST = lax.Precision.HIGHEST

D_MODEL = 2048
DEPTH = 4
PAST_LEN = 16384
PAGE_SIZE = 128
N_HEADS = 16
HEAD_DIM = 128
MOBA_BLOCK = 256
MOBA_TOP_K = 3
ROPE_THETA = 10000.0
M_HEADS = 8
M_QK_DIM = 128
M_V_DIM = 256
M_CHUNK = 64
IGATE_CAP = 15.0
D_FF = 5632
EPS = 1e-6

D_MQ = M_HEADS * M_QK_DIM
D_MV = M_HEADS * M_V_DIM
M_MAIN = 2 * D_MQ + 2 * D_MV
SAMPLE_T_PAD = 8
PAGES_PER_BLOCK = MOBA_BLOCK // PAGE_SIZE

TM_PROMPT = 512
TF_FFN = 512
TN_PROJ = 512
VMEM_LIMIT = 56 * 1024 * 1024


def _cparams(sem):
    return pltpu.CompilerParams(dimension_semantics=sem, vmem_limit_bytes=VMEM_LIMIT)


def _dot(a, b):
    return jnp.dot(a, b, preferred_element_type=F32)


def _dot_nt(a, b, precision=None):
    return lax.dot_general(a, b, (((1,), (1,)), ((), ())), precision=precision,
                           preferred_element_type=F32)


def _dot_tn(a, b):
    return lax.dot_general(a, b, (((0,), (0,)), ((), ())), preferred_element_type=F32)


def _rms_norm_rows(x, g):
    ms = jnp.mean(x * x, axis=-1, keepdims=True)
    return x * lax.rsqrt(ms + EPS) * g


def _ffn_kernel(x_ref, g_ref, wg_ref, wu_ref, wd_ref, o_ref, xn_ref, acc_ref):
    j = pl.program_id(1)

    @pl.when(j == 0)
    def _():
        xn_ref[...] = _rms_norm_rows(x_ref[...], g_ref[...]).astype(BF16)
        acc_ref[...] = jnp.zeros_like(acc_ref)

    xn = xn_ref[...]
    gate = _dot(xn, wg_ref[...])
    up = _dot(xn, wu_ref[...])
    h = (gate * jax.nn.sigmoid(gate) * up).astype(BF16)
    acc_ref[...] += _dot(h, wd_ref[...])

    @pl.when(j == pl.num_programs(1) - 1)
    def _():
        o_ref[...] = x_ref[...] + 0.5 * acc_ref[...]


def _ffn(x, g, w_up, w_down, layer, tm):
    t = x.shape[0]
    nj = D_FF // TF_FFN
    return pl.pallas_call(
        _ffn_kernel,
        grid=(t // tm, nj),
        in_specs=[
            pl.BlockSpec((tm, D_MODEL), lambda i, j: (i, 0)),
            pl.BlockSpec((None, 1, D_MODEL), lambda i, j: (layer, 0, 0)),
            pl.BlockSpec((None, D_MODEL, TF_FFN), lambda i, j: (layer, 0, j)),
            pl.BlockSpec((None, D_MODEL, TF_FFN), lambda i, j: (layer, 0, j + nj)),
            pl.BlockSpec((None, TF_FFN, D_MODEL), lambda i, j: (layer, j, 0)),
        ],
        out_specs=pl.BlockSpec((tm, D_MODEL), lambda i, j: (i, 0)),
        out_shape=jax.ShapeDtypeStruct((t, D_MODEL), F32),
        scratch_shapes=[pltpu.VMEM((tm, D_MODEL), BF16), pltpu.VMEM((tm, D_MODEL), F32)],
        compiler_params=_cparams(("parallel", "arbitrary")),
        name="ffn",
    )(x, g, w_up, w_up, w_down)


def _head_norm_rope(y, g, cos, sin):
    outs = []
    for hh in range(TN_PROJ // HEAD_DIM):
        yn = _rms_norm_rows(y[:, hh * HEAD_DIM:(hh + 1) * HEAD_DIM], g)
        outs.append(yn * cos + pltpu.roll(yn, HEAD_DIM // 2, 1) * sin)
    return jnp.concatenate(outs, axis=1)


def _qkv_kernel(x_ref, g_ref, w_ref, gq_ref, gk_ref, cos_ref, sin_ref, q_ref, k_ref, v_ref, xn_ref):
    j = pl.program_id(1)
    n_q = D_MODEL // TN_PROJ

    @pl.when(j == 0)
    def _():
        xn_ref[...] = _rms_norm_rows(x_ref[...], g_ref[...]).astype(BF16)

    y = _dot(xn_ref[...], w_ref[...])

    @pl.when(j < n_q)
    def _():
        q_ref[...] = _head_norm_rope(y, gq_ref[...], cos_ref[...], sin_ref[...])

    @pl.when((j >= n_q) & (j < 2 * n_q))
    def _():
        k_ref[...] = _head_norm_rope(y, gk_ref[...], cos_ref[...], sin_ref[...])

    @pl.when(j >= 2 * n_q)
    def _():
        v_ref[...] = y


def _qkv(x, g_mix, w_qkv, g_q, g_k, cos, sin, layer, j_attn, tm):
    t = x.shape[0]
    n_q = D_MODEL // TN_PROJ
    n_pos_blocks = cos.shape[0] // tm
    out_sd = jax.ShapeDtypeStruct((t, D_MODEL), F32)
    return pl.pallas_call(
        _qkv_kernel,
        grid=(t // tm, 3 * n_q),
        in_specs=[
            pl.BlockSpec((tm, D_MODEL), lambda i, j: (i, 0)),
            pl.BlockSpec((None, 1, D_MODEL), lambda i, j: (layer, 0, 0)),
            pl.BlockSpec((None, D_MODEL, TN_PROJ), lambda i, j: (j_attn, 0, j)),
            pl.BlockSpec((None, 1, HEAD_DIM), lambda i, j: (j_attn, 0, 0)),
            pl.BlockSpec((None, 1, HEAD_DIM), lambda i, j: (j_attn, 0, 0)),
            pl.BlockSpec((tm, HEAD_DIM), lambda i, j: (i % n_pos_blocks, 0)),
            pl.BlockSpec((tm, HEAD_DIM), lambda i, j: (i % n_pos_blocks, 0)),
        ],
        out_specs=[
            pl.BlockSpec((tm, TN_PROJ), lambda i, j: (i, jnp.minimum(j, n_q - 1))),
            pl.BlockSpec((tm, TN_PROJ), lambda i, j: (i, jnp.clip(j - n_q, 0, n_q - 1))),
            pl.BlockSpec((tm, TN_PROJ), lambda i, j: (i, jnp.clip(j - 2 * n_q, 0, n_q - 1))),
        ],
        out_shape=[out_sd, out_sd, out_sd],
        scratch_shapes=[pltpu.VMEM((tm, D_MODEL), BF16)],
        compiler_params=_cparams(("parallel", "arbitrary")),
        name="qkv",
    )(x, g_mix, w_qkv, g_q, g_k, cos, sin)


_MASK_BIAS = -(2.0 ** 100)


def _moba_prompt_kernel(q_ref, k_ref, v_ref, o_ref, qa_ref, ka_ref, vb_ref):
    seq = k_ref.shape[0]
    nb = seq // MOBA_BLOCK
    q = q_ref[...]
    k = k_ref[...]

    kmean = jnp.mean(k.reshape(nb, MOBA_BLOCK, HEAD_DIM), axis=1)
    gate_t = _dot_nt(kmean, q, precision=HIGHEST)
    blk = lax.broadcasted_iota(jnp.int32, gate_t.shape, 0)
    qblk = lax.broadcasted_iota(jnp.int32, gate_t.shape, 1) // MOBA_BLOCK
    cnt = jnp.zeros(gate_t.shape, jnp.int32)
    for kk in range(nb):
        gk = gate_t[kk:kk + 1, :]
        beats = (gk > gate_t) | ((gk == gate_t) & (kk < blk))
        cnt = cnt + jnp.where(beats & (kk < qblk), 1, 0)
    allowed = ((cnt < MOBA_TOP_K) & (blk < qblk)) | (blk == qblk)
    bias_t = jnp.where(allowed, 0.0, _MASK_BIAS)
    eye = (lax.broadcasted_iota(jnp.int32, (nb, HEAD_DIM), 0)
           == lax.broadcasted_iota(jnp.int32, (nb, HEAD_DIM), 1)).astype(F32)
    bias_q = _dot_tn(bias_t, eye)

    key_blk = lax.broadcasted_iota(jnp.int32, (seq, HEAD_DIM), 0) // MOBA_BLOCK
    key_onehot = key_blk == lax.broadcasted_iota(jnp.int32, (seq, HEAD_DIM), 1)
    qa_ref[:, 0:HEAD_DIM] = q.astype(BF16)
    qa_ref[:, HEAD_DIM:2 * HEAD_DIM] = bias_q.astype(BF16)
    ka_ref[:, 0:HEAD_DIM] = k.astype(BF16)
    ka_ref[:, HEAD_DIM:2 * HEAD_DIM] = key_onehot.astype(BF16)
    vb_ref[...] = v_ref[...].astype(BF16)

    row = lax.broadcasted_iota(jnp.int32, (MOBA_BLOCK, MOBA_BLOCK), 0)
    col = lax.broadcasted_iota(jnp.int32, (MOBA_BLOCK, MOBA_BLOCK), 1)
    causal = col <= row
    exp2_scale = (HEAD_DIM ** -0.5) * 1.4426950408889634
    for i in range(nb):
        lo, hi = i * MOBA_BLOCK, (i + 1) * MOBA_BLOCK
        s = _dot_nt(qa_ref[lo:hi, :], ka_ref[0:hi, :])
        own = jnp.where(causal, s[:, lo:hi], _MASK_BIAS)
        s = own if i == 0 else jnp.concatenate([s[:, 0:lo], own], axis=1)
        m = jnp.max(s, axis=-1, keepdims=True)
        p = jnp.exp2((s - m) * exp2_scale)
        l = jnp.sum(p, axis=-1, keepdims=True)
        o_ref[lo:hi, :] = (_dot(p.astype(BF16), vb_ref[0:hi, :]) / l).astype(o_ref.dtype)


def _moba_prompt(q, k, v, batch, seq):
    blk = pl.BlockSpec((seq, HEAD_DIM), lambda b, h: (b, h))
    return pl.pallas_call(
        _moba_prompt_kernel,
        grid=(batch, N_HEADS),
        in_specs=[blk, blk, blk],
        out_specs=blk,
        out_shape=jax.ShapeDtypeStruct((batch * seq, N_HEADS * HEAD_DIM), BF16),
        scratch_shapes=[
            pltpu.VMEM((seq, 2 * HEAD_DIM), BF16),
            pltpu.VMEM((seq, 2 * HEAD_DIM), BF16),
            pltpu.VMEM((seq, HEAD_DIM), BF16),
        ],
        compiler_params=_cparams(("parallel", "parallel")),
        name="moba_prompt",
    )(q, k, v)


def _proj_res_kernel(a_ref, w_ref, x_ref, o_ref):
    o_ref[...] = x_ref[...] + _dot(a_ref[...].astype(BF16), w_ref[...])


def _proj_res(a, w, x, j_w, tm):
    t = x.shape[0]
    return pl.pallas_call(
        _proj_res_kernel,
        grid=(t // tm,),
        in_specs=[
            pl.BlockSpec((tm, D_MODEL), lambda i: (i, 0)),
            pl.BlockSpec((None, D_MODEL, D_MODEL), lambda i: (j_w, 0, 0)),
            pl.BlockSpec((tm, D_MODEL), lambda i: (i, 0)),
        ],
        out_specs=pl.BlockSpec((tm, D_MODEL), lambda i: (i, 0)),
        out_shape=jax.ShapeDtypeStruct((t, D_MODEL), F32),
        compiler_params=_cparams(("parallel",)),
        name="proj_res",
    )(a, w, x)


def _mlstm_in_kernel(x_ref, g_ref, w_ref, wg_ref, y_ref, gates_ref, xn_ref):
    j = pl.program_id(1)

    @pl.when(j == 0)
    def _():
        xn = _rms_norm_rows(x_ref[...], g_ref[...]).astype(BF16)
        xn_ref[...] = xn
        gates_ref[...] = _dot(xn, wg_ref[...])

    y_ref[...] = _dot(xn_ref[...], w_ref[...])


def _mlstm_in(x, g_mix, w_main, w_gates, layer, j_m, tm):
    t = x.shape[0]
    return pl.pallas_call(
        _mlstm_in_kernel,
        grid=(t // tm, M_MAIN // TN_PROJ),
        in_specs=[
            pl.BlockSpec((tm, D_MODEL), lambda i, j: (i, 0)),
            pl.BlockSpec((None, 1, D_MODEL), lambda i, j: (layer, 0, 0)),
            pl.BlockSpec((None, D_MODEL, TN_PROJ), lambda i, j: (j_m, 0, j)),
            pl.BlockSpec((None, D_MODEL, 128), lambda i, j: (j_m, 0, 0)),
        ],
        out_specs=[
            pl.BlockSpec((tm, TN_PROJ), lambda i, j: (i, j)),
            pl.BlockSpec((tm, 128), lambda i, j: (i, 0)),
        ],
        out_shape=[jax.ShapeDtypeStruct((t, M_MAIN), F32), jax.ShapeDtypeStruct((t, 128), F32)],
        scratch_shapes=[pltpu.VMEM((tm, D_MODEL), BF16)],
        compiler_params=_cparams(("parallel", "arbitrary")),
        name="mlstm_in",
    )(x, g_mix, w_main, w_gates)


def _mlstm_chunk_kernel(l_valid, zero_init, *refs):
    if zero_init:
        main_ref, gates_ref, bias_ref, gh_ref = refs[:4]
        rest = refs[4:]
    else:
        main_ref, gates_ref, bias_ref, gh_ref, c0_ref, n0_ref, m0_ref = refs[:7]
        rest = refs[7:]
    hn_ref, cout_ref, nout_ref, mout_ref, c_s, n_s, m_s = rest
    c = pl.program_id(1)
    L = main_ref.shape[0]
    scale_k = M_QK_DIM ** -0.5

    @pl.when(c == 0)
    def _():
        if zero_init:
            c_s[...] = jnp.zeros_like(c_s)
            n_s[...] = jnp.zeros_like(n_s)
            m_s[...] = jnp.zeros_like(m_s)
        else:
            c_s[...] = c0_ref[...]
            n_s[...] = n0_ref[...]
            for h in range(M_HEADS):
                m_s[h] = jnp.broadcast_to(m0_ref[h], (8, 128))

    pre = gates_ref[...] + bias_ref[...]
    lane = lax.broadcasted_iota(jnp.int32, pre.shape, 1)
    act = jnp.where(lane < M_HEADS, IGATE_CAP * jnp.tanh(pre / IGATE_CAP), jax.nn.log_sigmoid(pre))
    r = lax.broadcasted_iota(jnp.int32, (L, L), 0)
    cc = lax.broadcasted_iota(jnp.int32, (L, L), 1)
    causal = cc <= r
    bcols = jnp.dot(causal.astype(F32), act, precision=HIGHEST, preferred_element_type=F32)
    act_t = act.T
    b_t = bcols.T
    rows = lax.broadcasted_iota(jnp.int32, (L, 1), 0)

    for h in range(M_HEADS):
        q = main_ref[:, h * M_QK_DIM:(h + 1) * M_QK_DIM]
        k = main_ref[:, D_MQ + h * M_QK_DIM:D_MQ + (h + 1) * M_QK_DIM]
        v = main_ref[:, 2 * D_MQ + h * M_V_DIM:2 * D_MQ + (h + 1) * M_V_DIM]
        o = main_ref[:, 2 * D_MQ + D_MV + h * M_V_DIM:2 * D_MQ + D_MV + (h + 1) * M_V_DIM]
        b_col = bcols[:, M_HEADS + h:M_HEADS + h + 1]
        b_row = b_t[M_HEADS + h:M_HEADS + h + 1, :]
        li_col = act[:, h:h + 1]
        li_row = act_t[h:h + 1, :]
        c0 = c_s[h]
        n0 = n_s[h]
        m0 = m_s[h][0:1, 0:1]

        dlog = jnp.where(causal, b_col - b_row + li_row, -jnp.inf)
        inter = b_col + m0
        m = jnp.maximum(inter, jnp.max(dlog, axis=-1, keepdims=True))
        w = jnp.exp(dlog - m)
        a = jnp.exp(inter - m)
        qb = q.astype(BF16)
        kb = k.astype(BF16)
        vb = v.astype(BF16)
        s = _dot_nt(qb, kb) * scale_k * w
        num = _dot(s.astype(BF16), vb) + a * _dot(qb, c0.astype(BF16))
        den = jnp.sum(s, axis=-1, keepdims=True) + a * jnp.sum(q * n0, axis=-1, keepdims=True)
        hh = num / jnp.maximum(jnp.abs(den), jnp.exp(-m))
        hn = hh * lax.rsqrt(jnp.mean(hh * hh, axis=-1, keepdims=True) + EPS)
        hn = hn * gh_ref[:, h * M_V_DIM:(h + 1) * M_V_DIM] * jax.nn.sigmoid(o)
        hn_ref[:, h * M_V_DIM:(h + 1) * M_V_DIM] = hn.astype(hn_ref.dtype)

        m_end = m[l_valid - 1:l_valid]
        a_end = a[l_valid - 1:l_valid]
        b_last = b_col[l_valid - 1:l_valid]
        wk = jnp.exp(b_last - b_col + li_col - m_end)
        if l_valid < L:
            wk = jnp.where(rows < l_valid, wk, 0.0)
        kw = k * (scale_k * wk)
        c_s[h] = a_end * c0 + _dot_tn(kw.astype(BF16), vb)
        n_s[h] = a_end * n0 + jnp.sum(kw, axis=0, keepdims=True)
        m_s[h] = jnp.broadcast_to(m_end, (8, 128))

    @pl.when(c == pl.num_programs(1) - 1)
    def _():
        cout_ref[...] = c_s[...]
        nout_ref[...] = n_s[...]
        mout_ref[...] = m_s[...]


def _mlstm_chunk(main, gates, bias, g_h, j_m, n_seq, n_chunks, chunk, l_valid, state=None):
    zero_init = state is None
    t = main.shape[0]
    in_specs = [
        pl.BlockSpec((chunk, M_MAIN), lambda n, c: (n * n_chunks + c, 0)),
        pl.BlockSpec((chunk, 128), lambda n, c: (n * n_chunks + c, 0)),
        pl.BlockSpec((None, 1, 128), lambda n, c: (j_m, 0, 0)),
        pl.BlockSpec((None, 1, D_MV), lambda n, c: (j_m, 0, 0)),
    ]
    args = [main, gates, bias, g_h]
    if not zero_init:
        in_specs += [
            pl.BlockSpec((None, None, M_HEADS, M_QK_DIM, M_V_DIM), lambda n, c: (j_m, n, 0, 0, 0)),
            pl.BlockSpec((None, None, M_HEADS, 1, M_QK_DIM), lambda n, c: (j_m, n, 0, 0, 0)),
            pl.BlockSpec((None, None, M_HEADS, 1, 1), lambda n, c: (j_m, n, 0, 0, 0)),
        ]
        args += list(state)
    hn_dtype = BF16 if chunk % 16 == 0 else F32
    return pl.pallas_call(
        functools.partial(_mlstm_chunk_kernel, l_valid, zero_init),
        grid=(n_seq, n_chunks),
        in_specs=in_specs,
        out_specs=[
            pl.BlockSpec((chunk, D_MV), lambda n, c: (n * n_chunks + c, 0)),
            pl.BlockSpec((None, M_HEADS, M_QK_DIM, M_V_DIM), lambda n, c: (n, 0, 0, 0)),
            pl.BlockSpec((None, M_HEADS, 1, M_QK_DIM), lambda n, c: (n, 0, 0, 0)),
            pl.BlockSpec((None, M_HEADS, 8, 128), lambda n, c: (n, 0, 0, 0)),
        ],
        out_shape=[
            jax.ShapeDtypeStruct((t, D_MV), hn_dtype),
            jax.ShapeDtypeStruct((n_seq, M_HEADS, M_QK_DIM, M_V_DIM), F32),
            jax.ShapeDtypeStruct((n_seq, M_HEADS, 1, M_QK_DIM), F32),
            jax.ShapeDtypeStruct((n_seq, M_HEADS, 8, 128), F32),
        ],
        scratch_shapes=[
            pltpu.VMEM((M_HEADS, M_QK_DIM, M_V_DIM), F32),
            pltpu.VMEM((M_HEADS, 1, M_QK_DIM), F32),
            pltpu.VMEM((M_HEADS, 8, 128), F32),
        ],
        compiler_params=_cparams(("parallel", "arbitrary")),
        name="mlstm_chunk",
    )(*args)


_BLOCKS_PER_STEP = 4
_PAGES_PER_STEP = _BLOCKS_PER_STEP * PAGES_PER_BLOCK


def _sample_select_kernel(pt_ref, q_ref, *refs):
    page_refs = refs[:_PAGES_PER_STEP]
    sel_ref, kmean_ref = refs[_PAGES_PER_STEP:]
    s = pl.program_id(1)
    c_past = kmean_ref.shape[1]

    for u in range(_BLOCKS_PER_STEP):
        tot = jnp.zeros((N_HEADS, HEAD_DIM), F32)
        for pp in range(PAGES_PER_BLOCK):
            tot = tot + jnp.sum(page_refs[u * PAGES_PER_BLOCK + pp][...], axis=0)
        tot = tot * (1.0 / MOBA_BLOCK)
        for h in range(N_HEADS):
            kmean_ref[h, pl.ds(s * _BLOCKS_PER_STEP + u, 1), :] = tot[h:h + 1, :]

    @pl.when(s == pl.num_programs(1) - 1)
    def _():
        out_lane = lax.broadcasted_iota(jnp.int32, sel_ref.shape, 1)
        out = jnp.zeros(sel_ref.shape, jnp.int32)
        for h in range(N_HEADS):
            g = _dot_nt(q_ref[:, h * HEAD_DIM:(h + 1) * HEAD_DIM], kmean_ref[h], precision=HIGHEST)
            lane = lax.broadcasted_iota(jnp.int32, g.shape, 1)
            for kk in range(MOBA_TOP_K):
                mx = jnp.max(g, axis=-1, keepdims=True)
                idx = jnp.min(jnp.where(g == mx, lane, c_past), axis=-1, keepdims=True)
                out = jnp.where(out_lane == h * MOBA_TOP_K + kk, idx, out)
                g = jnp.where(lane == idx, -jnp.inf, g)
        sel_ref[...] = out


def _sample_select(page_table, q, cache_k, j_attn, n_db):
    n_pages = page_table.shape[1]
    c_past = n_pages // PAGES_PER_BLOCK
    n_steps = c_past // _BLOCKS_PER_STEP

    def page_spec(u):
        return pl.BlockSpec((None, None, PAGE_SIZE, N_HEADS, HEAD_DIM),
                            lambda db, s, pt: (j_attn, pt[db, s * _PAGES_PER_STEP + u], 0, 0, 0))

    grid_spec = pltpu.PrefetchScalarGridSpec(
        num_scalar_prefetch=1,
        grid=(n_db, n_steps),
        in_specs=[pl.BlockSpec((SAMPLE_T_PAD, N_HEADS * HEAD_DIM), lambda db, s, pt: (db, 0))]
        + [page_spec(u) for u in range(_PAGES_PER_STEP)],
        out_specs=pl.BlockSpec((None, SAMPLE_T_PAD, 128), lambda db, s, pt: (db, 0, 0)),
        scratch_shapes=[pltpu.VMEM((N_HEADS, c_past, HEAD_DIM), F32)],
    )
    return pl.pallas_call(
        _sample_select_kernel,
        grid_spec=grid_spec,
        out_shape=jax.ShapeDtypeStruct((n_db, SAMPLE_T_PAD, 128), jnp.int32),
        compiler_params=_cparams(("parallel", "arbitrary")),
        name="sample_select",
    )(page_table, q, *([cache_k] * _PAGES_PER_STEP))


_N_SEL_PAGES = MOBA_TOP_K * PAGES_PER_BLOCK


def _sample_attn_kernel(t_valid, j_attn, pt_ref, sel_ref, q_ref, kn_ref, vn_ref, ck_ref, cv_ref, o_ref,
                        kbuf, vbuf, sem):
    n_heads = pl.num_programs(1)
    step = pl.program_id(0) * n_heads + pl.program_id(1)
    n_steps = pl.num_programs(0) * n_heads
    slot = step % 2
    scale = HEAD_DIM ** -0.5

    def copies(step_, slot_):
        db_ = step_ // n_heads
        h_ = step_ % n_heads
        out = []
        for t in range(t_valid):
            for kk in range(MOBA_TOP_K):
                blk = sel_ref[db_ * (SAMPLE_T_PAD * 128) + t * 128 + h_ * MOBA_TOP_K + kk]
                for pp in range(PAGES_PER_BLOCK):
                    page = pt_ref[db_, blk * PAGES_PER_BLOCK + pp]
                    i = (t * MOBA_TOP_K + kk) * PAGES_PER_BLOCK + pp
                    out.append(pltpu.make_async_copy(ck_ref.at[j_attn, page, :, h_, :], kbuf.at[slot_, i],
                                                     sem.at[0, slot_]))
                    out.append(pltpu.make_async_copy(cv_ref.at[j_attn, page, :, h_, :], vbuf.at[slot_, i],
                                                     sem.at[1, slot_]))
        return out

    @pl.when(step == 0)
    def _():
        for c in copies(step, slot):
            c.start()

    @pl.when(step + 1 < n_steps)
    def _():
        for c in copies(step + 1, 1 - slot):
            c.start()

    for c in copies(step, slot):
        c.wait()

    kn = kn_ref[...].astype(BF16)
    vn = vn_ref[...].astype(BF16)
    lane = lax.broadcasted_iota(jnp.int32, (SAMPLE_T_PAD, SAMPLE_T_PAD), 1)
    rows = []
    for t in range(t_valid):
        q = jnp.broadcast_to(q_ref[t:t + 1, :], (SAMPLE_T_PAD, HEAD_DIM)).astype(BF16)
        ks = kbuf[slot, t * _N_SEL_PAGES:(t + 1) * _N_SEL_PAGES].reshape(_N_SEL_PAGES * PAGE_SIZE, HEAD_DIM)
        vs = vbuf[slot, t * _N_SEL_PAGES:(t + 1) * _N_SEL_PAGES].reshape(_N_SEL_PAGES * PAGE_SIZE, HEAD_DIM)
        s_sel = _dot_nt(q, ks.astype(BF16)) * scale
        s_own = jnp.where(lane <= t, _dot_nt(q, kn) * scale, -jnp.inf)
        m = jnp.maximum(jnp.max(s_sel, axis=-1, keepdims=True), jnp.max(s_own, axis=-1, keepdims=True))
        p_sel = jnp.exp(s_sel - m)
        p_own = jnp.exp(s_own - m)
        l = jnp.sum(p_sel, axis=-1, keepdims=True) + jnp.sum(p_own, axis=-1, keepdims=True)
        out = (_dot(p_sel.astype(BF16), vs.astype(BF16)) + _dot(p_own.astype(BF16), vn)) / l
        rows.append(out[0:1, :])
    rows.append(jnp.zeros((SAMPLE_T_PAD - t_valid, HEAD_DIM), F32))
    o_ref[...] = jnp.concatenate(rows, axis=0)


def _sample_attn(page_table, sel_flat, q, k_new, v_new, cache_k, cache_v, j_attn, n_db, t_valid):
    row_spec = pl.BlockSpec((SAMPLE_T_PAD, HEAD_DIM), lambda db, h, pt, sel: (db, h))
    hbm_spec = pl.BlockSpec(memory_space=pl.ANY)
    n_slices = t_valid * _N_SEL_PAGES
    grid_spec = pltpu.PrefetchScalarGridSpec(
        num_scalar_prefetch=2,
        grid=(n_db, N_HEADS),
        in_specs=[row_spec, row_spec, row_spec, hbm_spec, hbm_spec],
        out_specs=row_spec,
        scratch_shapes=[
            pltpu.VMEM((2, n_slices, PAGE_SIZE, HEAD_DIM), F32),
            pltpu.VMEM((2, n_slices, PAGE_SIZE, HEAD_DIM), F32),
            pltpu.SemaphoreType.DMA((2, 2)),
        ],
    )
    return pl.pallas_call(
        functools.partial(_sample_attn_kernel, t_valid, j_attn),
        grid_spec=grid_spec,
        out_shape=jax.ShapeDtypeStruct((n_db * SAMPLE_T_PAD, N_HEADS * HEAD_DIM), F32),
        compiler_params=_cparams(("arbitrary", "arbitrary")),
        name="sample_attn",
    )(page_table, sel_flat, q, k_new, v_new, cache_k, cache_v)


def _rope_tables(pos):
    half = HEAD_DIM // 2
    inv = ROPE_THETA ** (-jnp.arange(half, dtype=F32) / half)
    ang = pos.astype(F32)[:, None] * inv[None, :]
    cos = jnp.cos(ang)
    sin = jnp.sin(ang)
    return jnp.concatenate([cos, cos], axis=-1), jnp.concatenate([-sin, sin], axis=-1)


def kernel(x_prompt, x_sample, cache_k, cache_v, state_C, state_n, state_m, page_table, g_ffn1, w_ffn1_up, w_ffn1_down, g_mix, attn_w_qkv, attn_g_q, attn_g_k, attn_w_o, mlstm_w_in, mlstm_b_if, mlstm_g_h, mlstm_w_out, g_ffn2, w_ffn2_up, w_ffn2_down):
    b_, s_, d = x_prompt.shape
    db, t_s, _ = x_sample.shape
    assert d == D_MODEL and s_ % MOBA_BLOCK == 0 and s_ % TM_PROMPT == 0 and t_s <= SAMPLE_T_PAD
    assert PAST_LEN % MOBA_BLOCK == 0 and page_table.shape[1] * PAGE_SIZE == PAST_LEN
    n_mlstm = mlstm_w_in.shape[0]

    w1u, w1d = w_ffn1_up.astype(BF16), w_ffn1_down.astype(BF16)
    w2u, w2d = w_ffn2_up.astype(BF16), w_ffn2_down.astype(BF16)
    wqkv, wo = attn_w_qkv.astype(BF16), attn_w_o.astype(BF16)
    w_in_main = mlstm_w_in[:, :, :M_MAIN].astype(BF16)
    w_in_gates = jnp.pad(mlstm_w_in[:, :, M_MAIN:], ((0, 0), (0, 0), (0, 128 - 2 * M_HEADS))).astype(BF16)
    w_out = mlstm_w_out.astype(BF16)
    g1 = g_ffn1.reshape(DEPTH, 1, d)
    g2 = g_ffn2.reshape(DEPTH, 1, d)
    gm = g_mix.reshape(DEPTH, 1, d)
    gq = attn_g_q.reshape(-1, 1, HEAD_DIM)
    gk = attn_g_k.reshape(-1, 1, HEAD_DIM)
    b_if = jnp.pad(mlstm_b_if, ((0, 0), (0, 128 - 2 * M_HEADS))).reshape(n_mlstm, 1, 128)
    g_h = mlstm_g_h.reshape(n_mlstm, 1, D_MV)

    cos_p, sin_p = _rope_tables(jnp.arange(s_))
    cos_s, sin_s = _rope_tables(PAST_LEN + (jnp.arange(db * SAMPLE_T_PAD) % SAMPLE_T_PAD))

    st_n = state_n.reshape(n_mlstm, db, M_HEADS, 1, M_QK_DIM)
    st_m = state_m.reshape(n_mlstm, db, M_HEADS, 1, 1)

    xp = x_prompt.reshape(b_ * s_, d)
    xs = jnp.pad(x_sample, ((0, 0), (0, SAMPLE_T_PAD - t_s), (0, 0))).reshape(db * SAMPLE_T_PAD, d)
    tm_s = db * SAMPLE_T_PAD

    kp_l, vp_l, ks_l, vs_l = [], [], [], []
    cp_l, np_l, mp_l, cs_l, ns_l, ms_l = [], [], [], [], [], []
    for i in range(DEPTH):
        xp = _ffn(xp, g1, w1u, w1d, i, TM_PROMPT)
        xs = _ffn(xs, g1, w1u, w1d, i, tm_s)
        j = i // 2
        if i % 2 == 0:
            qp, kp, vp = _qkv(xp, gm, wqkv, gq, gk, cos_p, sin_p, i, j, TM_PROMPT)
            op = _moba_prompt(qp, kp, vp, b_, s_)
            xp = _proj_res(op, wo, xp, j, TM_PROMPT)
            qs, ks, vs = _qkv(xs, gm, wqkv, gq, gk, cos_s, sin_s, i, j, tm_s)
            sel = _sample_select(page_table, qs, cache_k, j, db)
            os_ = _sample_attn(page_table, sel.reshape(-1), qs, ks, vs, cache_k, cache_v, j, db, t_s)
            xs = _proj_res(os_, wo, xs, j, tm_s)
            kp_l.append(kp.reshape(b_, s_ // PAGE_SIZE, PAGE_SIZE, N_HEADS, HEAD_DIM))
            vp_l.append(vp.reshape(b_, s_ // PAGE_SIZE, PAGE_SIZE, N_HEADS, HEAD_DIM))
            ks_l.append(ks.reshape(db, SAMPLE_T_PAD, N_HEADS, HEAD_DIM)[:, :t_s])
            vs_l.append(vs.reshape(db, SAMPLE_T_PAD, N_HEADS, HEAD_DIM)[:, :t_s])
        else:
            main, gates = _mlstm_in(xp, gm, w_in_main, w_in_gates, i, j, TM_PROMPT)
            hn, c_p, n_p, m_p = _mlstm_chunk(main, gates, b_if, g_h, j, b_, s_ // M_CHUNK, M_CHUNK, M_CHUNK)
            xp = _proj_res(hn, w_out, xp, j, TM_PROMPT)
            main, gates = _mlstm_in(xs, gm, w_in_main, w_in_gates, i, j, tm_s)
            hn, c_s, n_s, m_s = _mlstm_chunk(main, gates, b_if, g_h, j, db, 1, SAMPLE_T_PAD, t_s,
                                             state=(state_C, st_n, st_m))
            xs = _proj_res(hn, w_out, xs, j, tm_s)
            cp_l.append(c_p)
            np_l.append(n_p[:, :, 0, :])
            mp_l.append(m_p[:, :, 0, 0])
            cs_l.append(c_s)
            ns_l.append(n_s[:, :, 0, :])
            ms_l.append(m_s[:, :, 0, 0])
        xp = _ffn(xp, g2, w2u, w2d, i, TM_PROMPT)
        xs = _ffn(xs, g2, w2u, w2d, i, tm_s)

    y_prompt = xp.reshape(b_, s_, d)
    y_sample = xs.reshape(db, SAMPLE_T_PAD, d)[:, :t_s]
    return (y_prompt, y_sample, jnp.stack(kp_l), jnp.stack(vp_l), jnp.stack(ks_l), jnp.stack(vs_l),
            jnp.stack(cp_l), jnp.stack(np_l), jnp.stack(mp_l), jnp.stack(cs_l), jnp.stack(ns_l), jnp.stack(ms_l))
```

```python
import functools

import jax
import jax.numpy as jnp
from jax import lax
from jax.experimental import pallas as pl
from jax.experimental.pallas import tpu as pltpu

F32 = jnp.float32
BF16 = jnp.bfloat16
HIGHEST = lax.Precision.HIGHEST

D_MODEL = 2048
DEPTH = 4
PAST_LEN = 16384
PAGE_SIZE = 128
N_HEADS = 16
HEAD_DIM = 128
MOBA_BLOCK = 256
MOBA_TOP_K = 3
ROPE_THETA = 10000.0
M_HEADS = 8
M_QK_DIM = 128
M_V_DIM = 256
M_CHUNK = 256
IGATE_CAP = 15.0
D_FF = 5632
EPS = 1e-6

D_MQ = M_HEADS * M_QK_DIM
D_MV = M_HEADS * M_V_DIM
M_MAIN = 2 * D_MQ + 2 * D_MV
SAMPLE_T_PAD = 8
PAGES_PER_BLOCK = MOBA_BLOCK // PAGE_SIZE

TM_PROMPT = 512
TM_PROJ = 1024
TF_FFN = 512
TN_PROJ = 512
VMEM_LIMIT = 56 * 1024 * 1024


def _cparams(sem):
    return pltpu.CompilerParams(dimension_semantics=sem, vmem_limit_bytes=VMEM_LIMIT)


def _dot(a, b):
    return jnp.dot(a, b, preferred_element_type=F32)


def _dot_nt(a, b, precision=None):
    return lax.dot_general(a, b, (((1,), (1,)), ((), ())), precision=precision,
                           preferred_element_type=F32)


def _dot_tn(a, b):
    return lax.dot_general(a, b, (((0,), (0,)), ((), ())), preferred_element_type=F32)


def _rms_norm_rows(x, g):
    ms = jnp.mean(x * x, axis=-1, keepdims=True)
    return x * lax.rsqrt(ms + EPS) * g


_FFN_SIDE_PAGES = 2


def _ffn_kernel(n_side, *refs):
    if n_side:
        refs = refs[1:]
    x_ref, g_ref, wg_ref, wu_ref, wd_ref = refs[:5]
    page_refs = refs[5:5 + n_side]
    o_ref = refs[5 + n_side]
    rest = refs[6 + n_side:]
    if n_side:
        psum_ref, xn_ref, acc_ref = rest
    else:
        xn_ref, acc_ref = rest
    j = pl.program_id(1)

    @pl.when(j == 0)
    def _():
        xn_ref[...] = _rms_norm_rows(x_ref[...], g_ref[...]).astype(BF16)
        acc_ref[...] = jnp.zeros_like(acc_ref)

    xn = xn_ref[...]
    gate = _dot(xn, wg_ref[...])
    up = _dot(xn, wu_ref[...])
    h = (gate * jax.nn.sigmoid(gate) * up).astype(BF16)
    acc_ref[...] += _dot(h, wd_ref[...])

    for u in range(n_side):
        psum_ref[u] = jnp.sum(page_refs[u][...], axis=0)

    @pl.when(j == pl.num_programs(1) - 1)
    def _():
        o_ref[...] = x_ref[...] + 0.5 * acc_ref[...]


def _ffn(x, g, w_up, w_down, layer, tm, side=None):
    t = x.shape[0]
    nj = D_FF // TF_FFN
    n_steps = (t // tm) * nj
    in_specs = [
        pl.BlockSpec((tm, D_MODEL), lambda i, j, *_: (i, 0)),
        pl.BlockSpec((None, 1, D_MODEL), lambda i, j, *_: (layer, 0, 0)),
        pl.BlockSpec((None, D_MODEL, TF_FFN), lambda i, j, *_: (layer, 0, j)),
        pl.BlockSpec((None, D_MODEL, TF_FFN), lambda i, j, *_: (layer, 0, j + nj)),
        pl.BlockSpec((None, TF_FFN, D_MODEL), lambda i, j, *_: (layer, j, 0)),
    ]
    out_specs = pl.BlockSpec((tm, D_MODEL), lambda i, j, *_: (i, 0))
    out_shape = jax.ShapeDtypeStruct((t, D_MODEL), F32)
    scratch = [pltpu.VMEM((tm, D_MODEL), BF16), pltpu.VMEM((tm, D_MODEL), F32)]
    args = [x, g, w_up, w_up, w_down]
    if side is None:
        return pl.pallas_call(
            functools.partial(_ffn_kernel, 0),
            grid=(t // tm, nj), in_specs=in_specs, out_specs=out_specs, out_shape=out_shape,
            scratch_shapes=scratch, compiler_params=_cparams(("parallel", "arbitrary")), name="ffn",
        )(*args)

    pt_flat, cache_k, j_attn, first, count = side
    n_side = _FFN_SIDE_PAGES
    n_groups = -(-count // n_side)
    assert n_steps >= n_groups

    def page_spec(u):
        def index_map(i, j, pt):
            group = jnp.minimum(i * nj + j, n_groups - 1)
            return (j_attn, pt[first + jnp.minimum(group * n_side + u, count - 1)], 0, 0, 0)
        return pl.BlockSpec((None, None, PAGE_SIZE, N_HEADS, HEAD_DIM), index_map)

    def psum_map(i, j, pt):
        return (jnp.minimum(i * nj + j, n_groups - 1), 0, 0)

    grid_spec = pltpu.PrefetchScalarGridSpec(
        num_scalar_prefetch=1,
        grid=(t // tm, nj),
        in_specs=in_specs + [page_spec(u) for u in range(n_side)],
        out_specs=[out_specs, pl.BlockSpec((n_side, N_HEADS, HEAD_DIM), psum_map)],
        scratch_shapes=scratch,
    )
    y, psum = pl.pallas_call(
        functools.partial(_ffn_kernel, n_side),
        grid_spec=grid_spec,
        out_shape=[out_shape, jax.ShapeDtypeStruct((n_groups * n_side, N_HEADS, HEAD_DIM), F32)],
        compiler_params=_cparams(("arbitrary", "arbitrary")),
        name="ffn_pagesum",
    )(pt_flat, *args, *([cache_k] * n_side))
    return y, psum[:count]


_MXU_COLS = 256


def _dot_split(x, m):
    hi = x.astype(BF16)
    lo = (x - hi.astype(F32)).astype(BF16)
    return _dot(hi, m) + _dot(lo, m)


def _project_heads(xn_ref, w_ref, out_ref, g_ref, cos_ref, sin_ref):
    if g_ref is not None:
        r = lax.broadcasted_iota(jnp.int32, (_MXU_COLS, _MXU_COLS), 0)
        c = lax.broadcasted_iota(jnp.int32, (_MXU_COLS, _MXU_COLS), 1)
        same_head = (r // HEAD_DIM) == (c // HEAD_DIM)
        head_sum = same_head.astype(BF16)
        half_swap = (same_head & ((r % HEAD_DIM) == ((c + HEAD_DIM // 2) % HEAD_DIM))).astype(BF16)
        heads = _MXU_COLS // HEAD_DIM
        g = jnp.concatenate([g_ref[...]] * heads, axis=1)
    y_all = _dot(xn_ref[...], w_ref[...])
    if g_ref is None:
        out_ref[...] = y_all
        return
    for p in range(TN_PROJ // _MXU_COLS):
        y = y_all[:, p * _MXU_COLS:(p + 1) * _MXU_COLS]
        ms = _dot_split(y * y, head_sum) * (1.0 / HEAD_DIM)
        yn = y * lax.rsqrt(ms + EPS) * g
        rot = _dot_split(yn, half_swap)
        for hh in range(heads):
            lo = p * _MXU_COLS + hh * HEAD_DIM
            sl = slice(hh * HEAD_DIM, (hh + 1) * HEAD_DIM)
            out_ref[:, lo:lo + HEAD_DIM] = yn[:, sl] * cos_ref[...] + rot[:, sl] * sin_ref[...]


def _qkv_kernel(x_ref, g_ref, w_ref, gq_ref, gk_ref, cos_ref, sin_ref, q_ref, k_ref, v_ref, xn_ref):
    j = pl.program_id(1)
    n_q = D_MODEL // TN_PROJ

    @pl.when(j == 0)
    def _():
        xn_ref[...] = _rms_norm_rows(x_ref[...], g_ref[...]).astype(BF16)

    @pl.when(j < n_q)
    def _():
        _project_heads(xn_ref, w_ref, q_ref, gq_ref, cos_ref, sin_ref)

    @pl.when((j >= n_q) & (j < 2 * n_q))
    def _():
        _project_heads(xn_ref, w_ref, k_ref, gk_ref, cos_ref, sin_ref)

    @pl.when(j >= 2 * n_q)
    def _():
        _project_heads(xn_ref, w_ref, v_ref, None, None, None)


def _qkv(x, g_mix, w_qkv, g_q, g_k, cos, sin, layer, j_attn, tm):
    t = x.shape[0]
    n_q = D_MODEL // TN_PROJ
    n_pos_blocks = cos.shape[0] // tm
    out_sd = jax.ShapeDtypeStruct((t, D_MODEL), F32)
    return pl.pallas_call(
        _qkv_kernel,
        grid=(t // tm, 3 * n_q),
        in_specs=[
            pl.BlockSpec((tm, D_MODEL), lambda i, j: (i, 0)),
            pl.BlockSpec((None, 1, D_MODEL), lambda i, j: (layer, 0, 0)),
            pl.BlockSpec((None, D_MODEL, TN_PROJ), lambda i, j: (j_attn, 0, j)),
            pl.BlockSpec((None, 1, HEAD_DIM), lambda i, j: (j_attn, 0, 0)),
            pl.BlockSpec((None, 1, HEAD_DIM), lambda i, j: (j_attn, 0, 0)),
            pl.BlockSpec((tm, HEAD_DIM), lambda i, j: (i % n_pos_blocks, 0)),
            pl.BlockSpec((tm, HEAD_DIM), lambda i, j: (i % n_pos_blocks, 0)),
        ],
        out_specs=[
            pl.BlockSpec((tm, TN_PROJ), lambda i, j: (i, jnp.minimum(j, n_q - 1))),
            pl.BlockSpec((tm, TN_PROJ), lambda i, j: (i, jnp.clip(j - n_q, 0, n_q - 1))),
            pl.BlockSpec((tm, TN_PROJ), lambda i, j: (i, jnp.clip(j - 2 * n_q, 0, n_q - 1))),
        ],
        out_shape=[out_sd, out_sd, out_sd],
        scratch_shapes=[pltpu.VMEM((tm, D_MODEL), BF16)],
        compiler_params=_cparams(("parallel", "arbitrary")),
        name="qkv",
    )(x, g_mix, w_qkv, g_q, g_k, cos, sin)


_MASK_BIAS = -(2.0 ** 100)


def _moba_prompt_kernel(q_ref, k_ref, v_ref, o_ref, qa_ref, ka_ref, vb_ref):
    seq = k_ref.shape[0]
    nb = seq // MOBA_BLOCK
    q = q_ref[...]
    k = k_ref[...]

    kmean = jnp.mean(k.reshape(nb, MOBA_BLOCK, HEAD_DIM), axis=1)
    gate_t = _dot_nt(kmean, q, precision=HIGHEST)
    blk = lax.broadcasted_iota(jnp.int32, gate_t.shape, 0)
    qblk = lax.broadcasted_iota(jnp.int32, gate_t.shape, 1) // MOBA_BLOCK
    cnt = jnp.zeros(gate_t.shape, jnp.int32)
    for kk in range(nb):
        gk = gate_t[kk:kk + 1, :]
        beats = (gk > gate_t) | ((gk == gate_t) & (kk < blk))
        cnt = cnt + jnp.where(beats & (kk < qblk), 1, 0)
    allowed = ((cnt < MOBA_TOP_K) & (blk < qblk)) | (blk == qblk)
    bias_t = jnp.where(allowed, 0.0, _MASK_BIAS)
    eye = (lax.broadcasted_iota(jnp.int32, (nb, HEAD_DIM), 0)
           == lax.broadcasted_iota(jnp.int32, (nb, HEAD_DIM), 1)).astype(F32)
    bias_q = _dot_tn(bias_t, eye)

    key_blk = lax.broadcasted_iota(jnp.int32, (seq, HEAD_DIM), 0) // MOBA_BLOCK
    key_onehot = key_blk == lax.broadcasted_iota(jnp.int32, (seq, HEAD_DIM), 1)
    qa_ref[:, 0:HEAD_DIM] = q.astype(BF16)
    qa_ref[:, HEAD_DIM:2 * HEAD_DIM] = bias_q.astype(BF16)
    ka_ref[:, 0:HEAD_DIM] = k.astype(BF16)
    ka_ref[:, HEAD_DIM:2 * HEAD_DIM] = key_onehot.astype(BF16)
    vb_ref[...] = v_ref[...].astype(BF16)

    row = lax.broadcasted_iota(jnp.int32, (MOBA_BLOCK, MOBA_BLOCK), 0)
    col = lax.broadcasted_iota(jnp.int32, (MOBA_BLOCK, MOBA_BLOCK), 1)
    causal = col <= row
    exp2_scale = (HEAD_DIM ** -0.5) * 1.4426950408889634
    for i in range(nb):
        lo, hi = i * MOBA_BLOCK, (i + 1) * MOBA_BLOCK
        s = _dot_nt(qa_ref[lo:hi, :], ka_ref[0:hi, :])
        own = jnp.where(causal, s[:, lo:hi], _MASK_BIAS)
        s = own if i == 0 else jnp.concatenate([s[:, 0:lo], own], axis=1)
        m = jnp.max(s, axis=-1, keepdims=True)
        p = jnp.exp2((s - m) * exp2_scale)
        l = jnp.sum(p, axis=-1, keepdims=True)
        o_ref[lo:hi, :] = (_dot(p.astype(BF16), vb_ref[0:hi, :]) / l).astype(o_ref.dtype)


def _moba_prompt(q, k, v, batch, seq):
    blk = pl.BlockSpec((seq, HEAD_DIM), lambda b, h: (b, h))
    return pl.pallas_call(
        _moba_prompt_kernel,
        grid=(batch, N_HEADS),
        in_specs=[blk, blk, blk],
        out_specs=blk,
        out_shape=jax.ShapeDtypeStruct((batch * seq, N_HEADS * HEAD_DIM), BF16),
        scratch_shapes=[
            pltpu.VMEM((seq, 2 * HEAD_DIM), BF16),
            pltpu.VMEM((seq, 2 * HEAD_DIM), BF16),
            pltpu.VMEM((seq, HEAD_DIM), BF16),
        ],
        compiler_params=_cparams(("parallel", "parallel")),
        name="moba_prompt",
    )(q, k, v)


def _proj_res_kernel(a_ref, w_ref, x_ref, o_ref):
    o_ref[...] = x_ref[...] + _dot(a_ref[...].astype(BF16), w_ref[...])


def _proj_res(a, w, x, j_w, tm):
    t = x.shape[0]
    return pl.pallas_call(
        _proj_res_kernel,
        grid=(t // tm,),
        in_specs=[
            pl.BlockSpec((tm, D_MODEL), lambda i: (i, 0)),
            pl.BlockSpec((None, D_MODEL, D_MODEL), lambda i: (j_w, 0, 0)),
            pl.BlockSpec((tm, D_MODEL), lambda i: (i, 0)),
        ],
        out_specs=pl.BlockSpec((tm, D_MODEL), lambda i: (i, 0)),
        out_shape=jax.ShapeDtypeStruct((t, D_MODEL), F32),
        compiler_params=_cparams(("parallel",)),
        name="proj_res",
    )(a, w, x)


def _mlstm_in_kernel(x_ref, g_ref, w_ref, wg_ref, y_ref, gates_ref, xn_ref):
    j = pl.program_id(1)

    @pl.when(j == 0)
    def _():
        xn = _rms_norm_rows(x_ref[...], g_ref[...]).astype(BF16)
        xn_ref[...] = xn
        gates_ref[...] = _dot(xn, wg_ref[...])

    y_ref[...] = _dot(xn_ref[...], w_ref[...])


def _mlstm_in(x, g_mix, w_main, w_gates, layer, j_m, tm):
    t = x.shape[0]
    return pl.pallas_call(
        _mlstm_in_kernel,
        grid=(t // tm, M_MAIN // TN_PROJ),
        in_specs=[
            pl.BlockSpec((tm, D_MODEL), lambda i, j: (i, 0)),
            pl.BlockSpec((None, 1, D_MODEL), lambda i, j: (layer, 0, 0)),
            pl.BlockSpec((None, D_MODEL, TN_PROJ), lambda i, j: (j_m, 0, j)),
            pl.BlockSpec((None, D_MODEL, 128), lambda i, j: (j_m, 0, 0)),
        ],
        out_specs=[
            pl.BlockSpec((tm, TN_PROJ), lambda i, j: (i, j)),
            pl.BlockSpec((tm, 128), lambda i, j: (i, 0)),
        ],
        out_shape=[jax.ShapeDtypeStruct((t, M_MAIN), F32), jax.ShapeDtypeStruct((t, 128), F32)],
        scratch_shapes=[pltpu.VMEM((tm, D_MODEL), BF16)],
        compiler_params=_cparams(("parallel", "arbitrary")),
        name="mlstm_in",
    )(x, g_mix, w_main, w_gates)


def _mlstm_chunk_kernel(l_valid, zero_init, *refs):
    if zero_init:
        main_ref, gates_ref, bias_ref, gh_ref = refs[:4]
        rest = refs[4:]
    else:
        main_ref, gates_ref, bias_ref, gh_ref, c0_ref, n0_ref, m0_ref = refs[:7]
        rest = refs[7:]
    hn_ref, cout_ref, nout_ref, mout_ref, c_s, n_s, m_s = rest
    c = pl.program_id(1)
    L = main_ref.shape[0]
    scale_k = M_QK_DIM ** -0.5

    @pl.when(c == 0)
    def _():
        if zero_init:
            c_s[...] = jnp.zeros_like(c_s)
            n_s[...] = jnp.zeros_like(n_s)
            m_s[...] = jnp.zeros_like(m_s)
        else:
            c_s[...] = c0_ref[...]
            n_s[...] = n0_ref[...]
            for h in range(M_HEADS):
                m_s[h] = jnp.broadcast_to(m0_ref[h], (8, 128))

    pre = gates_ref[...] + bias_ref[...]
    lane = lax.broadcasted_iota(jnp.int32, pre.shape, 1)
    act = jnp.where(lane < M_HEADS, IGATE_CAP * jnp.tanh(pre / IGATE_CAP), jax.nn.log_sigmoid(pre))
    r = lax.broadcasted_iota(jnp.int32, (L, L), 0)
    cc = lax.broadcasted_iota(jnp.int32, (L, L), 1)
    causal = cc <= r
    bcols = jnp.dot(causal.astype(F32), act, precision=HIGHEST, preferred_element_type=F32)
    act_t = act.T
    b_t = bcols.T
    rows = lax.broadcasted_iota(jnp.int32, (L, 1), 0)

    for h in range(M_HEADS):
        q = main_ref[:, h * M_QK_DIM:(h + 1) * M_QK_DIM]
        k = main_ref[:, D_MQ + h * M_QK_DIM:D_MQ + (h + 1) * M_QK_DIM]
        v = main_ref[:, 2 * D_MQ + h * M_V_DIM:2 * D_MQ + (h + 1) * M_V_DIM]
        o = main_ref[:, 2 * D_MQ + D_MV + h * M_V_DIM:2 * D_MQ + D_MV + (h + 1) * M_V_DIM]
        b_col = bcols[:, M_HEADS + h:M_HEADS + h + 1]
        b_row = b_t[M_HEADS + h:M_HEADS + h + 1, :]
        li_col = act[:, h:h + 1]
        li_row = act_t[h:h + 1, :]
        c0 = c_s[h]
        n0 = n_s[h]
        m0 = m_s[h][0:1, 0:1]

        dlog = jnp.where(causal, b_col - b_row + li_row, -jnp.inf)
        inter = b_col + m0
        m = jnp.maximum(inter, jnp.max(dlog, axis=-1, keepdims=True))
        w = jnp.exp(dlog - m)
        a = jnp.exp(inter - m)
        qb = q.astype(BF16)
        kb = k.astype(BF16)
        vb = v.astype(BF16)
        s = _dot_nt(qb, kb) * scale_k * w
        num = _dot(s.astype(BF16), vb) + a * _dot(qb, c0.astype(BF16))
        den = jnp.sum(s, axis=-1, keepdims=True) + a * jnp.sum(q * n0, axis=-1, keepdims=True)
        hh = num / jnp.maximum(jnp.abs(den), jnp.exp(-m))
        hn = hh * lax.rsqrt(jnp.mean(hh * hh, axis=-1, keepdims=True) + EPS)
        hn = hn * gh_ref[:, h * M_V_DIM:(h + 1) * M_V_DIM] * jax.nn.sigmoid(o)
        hn_ref[:, h * M_V_DIM:(h + 1) * M_V_DIM] = hn.astype(hn_ref.dtype)

        m_end = m[l_valid - 1:l_valid]
        a_end = a[l_valid - 1:l_valid]
        b_last = b_col[l_valid - 1:l_valid]
        wk = jnp.exp(b_last - b_col + li_col - m_end)
        if l_valid < L:
            wk = jnp.where(rows < l_valid, wk, 0.0)
        kw = k * (scale_k * wk)
        c_s[h] = a_end * c0 + _dot_tn(kw.astype(BF16), vb)
        n_s[h] = a_end * n0 + jnp.sum(kw, axis=0, keepdims=True)
        m_s[h] = jnp.broadcast_to(m_end, (8, 128))

    @pl.when(c == pl.num_programs(1) - 1)
    def _():
        cout_ref[...] = c_s[...]
        nout_ref[...] = n_s[...]
        mout_ref[...] = m_s[...]


def _mlstm_chunk(main, gates, bias, g_h, j_m, n_seq, n_chunks, chunk, l_valid, state=None):
    zero_init = state is None
    t = main.shape[0]
    in_specs = [
        pl.BlockSpec((chunk, M_MAIN), lambda n, c: (n * n_chunks + c, 0)),
        pl.BlockSpec((chunk, 128), lambda n, c: (n * n_chunks + c, 0)),
        pl.BlockSpec((None, 1, 128), lambda n, c: (j_m, 0, 0)),
        pl.BlockSpec((None, 1, D_MV), lambda n, c: (j_m, 0, 0)),
    ]
    args = [main, gates, bias, g_h]
    if not zero_init:
        in_specs += [
            pl.BlockSpec((None, None, M_HEADS, M_QK_DIM, M_V_DIM), lambda n, c: (j_m, n, 0, 0, 0)),
            pl.BlockSpec((None, None, M_HEADS, 1, M_QK_DIM), lambda n, c: (j_m, n, 0, 0, 0)),
            pl.BlockSpec((None, None, M_HEADS, 1, 1), lambda n, c: (j_m, n, 0, 0, 0)),
        ]
        args += list(state)
    hn_dtype = BF16 if chunk % 16 == 0 else F32
    return pl.pallas_call(
        functools.partial(_mlstm_chunk_kernel, l_valid, zero_init),
        grid=(n_seq, n_chunks),
        in_specs=in_specs,
        out_specs=[
            pl.BlockSpec((chunk, D_MV), lambda n, c: (n * n_chunks + c, 0)),
            pl.BlockSpec((None, M_HEADS, M_QK_DIM, M_V_DIM), lambda n, c: (n, 0, 0, 0)),
            pl.BlockSpec((None, M_HEADS, 1, M_QK_DIM), lambda n, c: (n, 0, 0, 0)),
            pl.BlockSpec((None, M_HEADS, 8, 128), lambda n, c: (n, 0, 0, 0)),
        ],
        out_shape=[
            jax.ShapeDtypeStruct((t, D_MV), hn_dtype),
            jax.ShapeDtypeStruct((n_seq, M_HEADS, M_QK_DIM, M_V_DIM), F32),
            jax.ShapeDtypeStruct((n_seq, M_HEADS, 1, M_QK_DIM), F32),
            jax.ShapeDtypeStruct((n_seq, M_HEADS, 8, 128), F32),
        ],
        scratch_shapes=[
            pltpu.VMEM((M_HEADS, M_QK_DIM, M_V_DIM), F32),
            pltpu.VMEM((M_HEADS, 1, M_QK_DIM), F32),
            pltpu.VMEM((M_HEADS, 8, 128), F32),
        ],
        compiler_params=_cparams(("parallel", "arbitrary")),
        name="mlstm_chunk",
    )(*args)


def _sample_select_kernel(q_ref, psum_ref, sel_ref, kmean_ref):
    c_past = kmean_ref.shape[1]

    for c in range(c_past):
        tot = psum_ref[c * PAGES_PER_BLOCK]
        for pp in range(1, PAGES_PER_BLOCK):
            tot = tot + psum_ref[c * PAGES_PER_BLOCK + pp]
        tot = tot * (1.0 / MOBA_BLOCK)
        for h in range(N_HEADS):
            kmean_ref[h, c:c + 1, :] = tot[h:h + 1, :]

    out_lane = lax.broadcasted_iota(jnp.int32, sel_ref.shape, 1)
    out = jnp.zeros(sel_ref.shape, jnp.int32)
    for h in range(N_HEADS):
        g = _dot_nt(q_ref[:, h * HEAD_DIM:(h + 1) * HEAD_DIM], kmean_ref[h], precision=HIGHEST)
        lane = lax.broadcasted_iota(jnp.int32, g.shape, 1)
        for kk in range(MOBA_TOP_K):
            mx = jnp.max(g, axis=-1, keepdims=True)
            idx = jnp.min(jnp.where(g == mx, lane, c_past), axis=-1, keepdims=True)
            out = jnp.where(out_lane == h * MOBA_TOP_K + kk, idx, out)
            g = jnp.where(lane == idx, -jnp.inf, g)
    sel_ref[...] = out


def _sample_select(psum, q, n_db):
    n_pages = psum.shape[1]
    c_past = n_pages // PAGES_PER_BLOCK
    return pl.pallas_call(
        _sample_select_kernel,
        grid=(n_db,),
        in_specs=[
            pl.BlockSpec((SAMPLE_T_PAD, N_HEADS * HEAD_DIM), lambda db: (db, 0)),
            pl.BlockSpec((None, n_pages, N_HEADS, HEAD_DIM), lambda db: (db, 0, 0, 0)),
        ],
        out_specs=pl.BlockSpec((None, SAMPLE_T_PAD, 128), lambda db: (db, 0, 0)),
        out_shape=jax.ShapeDtypeStruct((n_db, SAMPLE_T_PAD, 128), jnp.int32),
        scratch_shapes=[pltpu.VMEM((N_HEADS, c_past, HEAD_DIM), F32)],
        compiler_params=_cparams(("parallel",)),
        name="sample_select",
    )(q, psum)


_N_SEL_PAGES = MOBA_TOP_K * PAGES_PER_BLOCK


def _sample_attn_kernel(t_valid, j_attn, pt_ref, sel_ref, q_ref, kn_ref, vn_ref, ck_ref, cv_ref, o_ref,
                        kbuf, vbuf, sem):
    n_heads = pl.num_programs(1)
    step = pl.program_id(0) * n_heads + pl.program_id(1)
    n_steps = pl.num_programs(0) * n_heads
    slot = step % 2
    scale = HEAD_DIM ** -0.5

    def copies(step_, slot_):
        db_ = step_ // n_heads
        h_ = step_ % n_heads
        out = []
        for t in range(t_valid):
            for kk in range(MOBA_TOP_K):
                blk = sel_ref[db_ * (SAMPLE_T_PAD * 128) + t * 128 + h_ * MOBA_TOP_K + kk]
                for pp in range(PAGES_PER_BLOCK):
                    page = pt_ref[db_, blk * PAGES_PER_BLOCK + pp]
                    i = (t * MOBA_TOP_K + kk) * PAGES_PER_BLOCK + pp
                    out.append(pltpu.make_async_copy(ck_ref.at[j_attn, page, :, h_, :], kbuf.at[slot_, i],
                                                     sem.at[0, slot_]))
                    out.append(pltpu.make_async_copy(cv_ref.at[j_attn, page, :, h_, :], vbuf.at[slot_, i],
                                                     sem.at[1, slot_]))
        return out

    @pl.when(step == 0)
    def _():
        for c in copies(step, slot):
            c.start()

    @pl.when(step + 1 < n_steps)
    def _():
        for c in copies(step + 1, 1 - slot):
            c.start()

    for c in copies(step, slot):
        c.wait()

    kn = kn_ref[...].astype(BF16)
    vn = vn_ref[...].astype(BF16)
    lane = lax.broadcasted_iota(jnp.int32, (SAMPLE_T_PAD, SAMPLE_T_PAD), 1)
    rows = []
    for t in range(t_valid):
        q = jnp.broadcast_to(q_ref[t:t + 1, :], (SAMPLE_T_PAD, HEAD_DIM)).astype(BF16)
        ks = kbuf[slot, t * _N_SEL_PAGES:(t + 1) * _N_SEL_PAGES].reshape(_N_SEL_PAGES * PAGE_SIZE, HEAD_DIM)
        vs = vbuf[slot, t * _N_SEL_PAGES:(t + 1) * _N_SEL_PAGES].reshape(_N_SEL_PAGES * PAGE_SIZE, HEAD_DIM)
        s_sel = _dot_nt(q, ks.astype(BF16)) * scale
        s_own = jnp.where(lane <= t, _dot_nt(q, kn) * scale, -jnp.inf)
        m = jnp.maximum(jnp.max(s_sel, axis=-1, keepdims=True), jnp.max(s_own, axis=-1, keepdims=True))
        p_sel = jnp.exp(s_sel - m)
        p_own = jnp.exp(s_own - m)
        l = jnp.sum(p_sel, axis=-1, keepdims=True) + jnp.sum(p_own, axis=-1, keepdims=True)
        out = (_dot(p_sel.astype(BF16), vs.astype(BF16)) + _dot(p_own.astype(BF16), vn)) / l
        rows.append(out[0:1, :])
    rows.append(jnp.zeros((SAMPLE_T_PAD - t_valid, HEAD_DIM), F32))
    o_ref[...] = jnp.concatenate(rows, axis=0)


def _sample_attn(page_table, sel_flat, q, k_new, v_new, cache_k, cache_v, j_attn, n_db, t_valid):
    row_spec = pl.BlockSpec((SAMPLE_T_PAD, HEAD_DIM), lambda db, h, pt, sel: (db, h))
    hbm_spec = pl.BlockSpec(memory_space=pl.ANY)
    n_slices = t_valid * _N_SEL_PAGES
    grid_spec = pltpu.PrefetchScalarGridSpec(
        num_scalar_prefetch=2,
        grid=(n_db, N_HEADS),
        in_specs=[row_spec, row_spec, row_spec, hbm_spec, hbm_spec],
        out_specs=row_spec,
        scratch_shapes=[
            pltpu.VMEM((2, n_slices, PAGE_SIZE, HEAD_DIM), F32),
            pltpu.VMEM((2, n_slices, PAGE_SIZE, HEAD_DIM), F32),
            pltpu.SemaphoreType.DMA((2, 2)),
        ],
    )
    return pl.pallas_call(
        functools.partial(_sample_attn_kernel, t_valid, j_attn),
        grid_spec=grid_spec,
        out_shape=jax.ShapeDtypeStruct((n_db * SAMPLE_T_PAD, N_HEADS * HEAD_DIM), F32),
        compiler_params=_cparams(("arbitrary", "arbitrary")),
        name="sample_attn",
    )(page_table, sel_flat, q, k_new, v_new, cache_k, cache_v)


def _rope_tables(pos):
    half = HEAD_DIM // 2
    inv = ROPE_THETA ** (-jnp.arange(half, dtype=F32) / half)
    ang = pos.astype(F32)[:, None] * inv[None, :]
    cos = jnp.cos(ang)
    sin = jnp.sin(ang)
    return jnp.concatenate([cos, cos], axis=-1), jnp.concatenate([-sin, sin], axis=-1)


def kernel(x_prompt, x_sample, cache_k, cache_v, state_C, state_n, state_m, page_table, g_ffn1, w_ffn1_up, w_ffn1_down, g_mix, attn_w_qkv, attn_g_q, attn_g_k, attn_w_o, mlstm_w_in, mlstm_b_if, mlstm_g_h, mlstm_w_out, g_ffn2, w_ffn2_up, w_ffn2_down):
    b_, s_, d = x_prompt.shape
    db, t_s, _ = x_sample.shape
    assert d == D_MODEL and s_ % MOBA_BLOCK == 0 and s_ % TM_PROJ == 0 and s_ % M_CHUNK == 0 and t_s <= SAMPLE_T_PAD
    assert PAST_LEN % MOBA_BLOCK == 0 and page_table.shape[1] * PAGE_SIZE == PAST_LEN
    n_mlstm = mlstm_w_in.shape[0]

    w1u, w1d = w_ffn1_up.astype(BF16), w_ffn1_down.astype(BF16)
    w2u, w2d = w_ffn2_up.astype(BF16), w_ffn2_down.astype(BF16)
    wqkv, wo = attn_w_qkv.astype(BF16), attn_w_o.astype(BF16)
    w_in_main = mlstm_w_in[:, :, :M_MAIN].astype(BF16)
    w_in_gates = jnp.pad(mlstm_w_in[:, :, M_MAIN:], ((0, 0), (0, 0), (0, 128 - 2 * M_HEADS))).astype(BF16)
    w_out = mlstm_w_out.astype(BF16)
    g1 = g_ffn1.reshape(DEPTH, 1, d)
    g2 = g_ffn2.reshape(DEPTH, 1, d)
    gm = g_mix.reshape(DEPTH, 1, d)
    gq = attn_g_q.reshape(-1, 1, HEAD_DIM)
    gk = attn_g_k.reshape(-1, 1, HEAD_DIM)
    b_if = jnp.pad(mlstm_b_if, ((0, 0), (0, 128 - 2 * M_HEADS))).reshape(n_mlstm, 1, 128)
    g_h = mlstm_g_h.reshape(n_mlstm, 1, D_MV)

    cos_p, sin_p = _rope_tables(jnp.arange(s_))
    cos_s, sin_s = _rope_tables(PAST_LEN + (jnp.arange(db * SAMPLE_T_PAD) % SAMPLE_T_PAD))

    st_n = state_n.reshape(n_mlstm, db, M_HEADS, 1, M_QK_DIM)
    st_m = state_m.reshape(n_mlstm, db, M_HEADS, 1, 1)

    xp = x_prompt.reshape(b_ * s_, d)
    xs = jnp.pad(x_sample, ((0, 0), (0, SAMPLE_T_PAD - t_s), (0, 0))).reshape(db * SAMPLE_T_PAD, d)
    tm_s = db * SAMPLE_T_PAD

    n_attn = cache_k.shape[0]
    pt_flat = page_table.reshape(-1)
    pages_per_layer = pt_flat.shape[0]
    pages_per_call = n_attn * pages_per_layer // (2 * DEPTH)
    calls_per_layer = pages_per_layer // pages_per_call
    assert pages_per_call * 2 * DEPTH == n_attn * pages_per_layer and calls_per_layer * pages_per_call == pages_per_layer
    psums = [[] for _ in range(n_attn)]

    def ffn_prompt(x, g, w_up, w_down, i, call):
        j_attn = call // calls_per_layer
        side = (pt_flat, cache_k, j_attn, (call % calls_per_layer) * pages_per_call, pages_per_call)
        y, ps = _ffn(x, g, w_up, w_down, i, TM_PROMPT, side=side)
        psums[j_attn].append(ps)
        return y

    kp_l, vp_l, ks_l, vs_l = [], [], [], []
    cp_l, np_l, mp_l, cs_l, ns_l, ms_l = [], [], [], [], [], []
    for i in range(DEPTH):
        xp = ffn_prompt(xp, g1, w1u, w1d, i, 2 * i)
        j = i // 2
        if i % 2 == 0:
            qp, kp, vp = _qkv(xp, gm, wqkv, gq, gk, cos_p, sin_p, i, j, TM_PROJ)
            op = _moba_prompt(qp, kp, vp, b_, s_)
            xp = _proj_res(op, wo, xp, j, TM_PROMPT)
            kp_l.append(kp.reshape(b_, s_ // PAGE_SIZE, PAGE_SIZE, N_HEADS, HEAD_DIM))
            vp_l.append(vp.reshape(b_, s_ // PAGE_SIZE, PAGE_SIZE, N_HEADS, HEAD_DIM))
        else:
            main, gates = _mlstm_in(xp, gm, w_in_main, w_in_gates, i, j, TM_PROJ)
            hn, c_p, n_p, m_p = _mlstm_chunk(main, gates, b_if, g_h, j, b_, s_ // M_CHUNK, M_CHUNK, M_CHUNK)
            xp = _proj_res(hn, w_out, xp, j, TM_PROMPT)
            cp_l.append(c_p)
            np_l.append(n_p[:, :, 0, :])
            mp_l.append(m_p[:, :, 0, 0])
        xp = ffn_prompt(xp, g2, w2u, w2d, i, 2 * i + 1)

    psum_l = [jnp.concatenate(p, axis=0).reshape(db, -1, N_HEADS, HEAD_DIM) for p in psums]
    for i in range(DEPTH):
        xs = _ffn(xs, g1, w1u, w1d, i, tm_s)
        j = i // 2
        if i % 2 == 0:
            qs, ks, vs = _qkv(xs, gm, wqkv, gq, gk, cos_s, sin_s, i, j, tm_s)
            sel = _sample_select(psum_l[j], qs, db)
            os_ = _sample_attn(page_table, sel.reshape(-1), qs, ks, vs, cache_k, cache_v, j, db, t_s)
            xs = _proj_res(os_, wo, xs, j, tm_s)
            ks_l.append(ks.reshape(db, SAMPLE_T_PAD, N_HEADS, HEAD_DIM)[:, :t_s])
            vs_l.append(vs.reshape(db, SAMPLE_T_PAD, N_HEADS, HEAD_DIM)[:, :t_s])
        else:
            main, gates = _mlstm_in(xs, gm, w_in_main, w_in_gates, i, j, tm_s)
            hn, c_s, n_s, m_s = _mlstm_chunk(main, gates, b_if, g_h, j, db, 1, SAMPLE_T_PAD, t_s,
                                             state=(state_C, st_n, st_m))
            xs = _proj_res(hn, w_out, xs, j, tm_s)
            cs_l.append(c_s)
            ns_l.append(n_s[:, :, 0, :])
            ms_l.append(m_s[:, :, 0, 0])
        xs = _ffn(xs, g2, w2u, w2d, i, tm_s)

    y_prompt = xp.reshape(b_, s_, d)
    y_sample = xs.reshape(db, SAMPLE_T_PAD, d)[:, :t_s]
    return (y_prompt, y_sample, jnp.stack(kp_l), jnp.stack(vp_l), jnp.stack(ks_l), jnp.stack(vs_l),
            jnp.stack(cp_l), jnp.stack(np_l), jnp.stack(mp_l), jnp.stack(cs_l), jnp.stack(ns_l), jnp.stack(ms_l))
```

```python
import functools

import jax
import jax.numpy as jnp
from jax import lax
from jax.experimental import pallas as pl
from jax.experimental.pallas import tpu as pltpu

F32 = jnp.float32
BF16 = jnp.bfloat16
HIGHEST = lax.Precision.HIGHEST

D_MODEL = 2048
DEPTH = 4
PAST_LEN = 16384
PAGE_SIZE = 128
N_HEADS = 16
HEAD_DIM = 128
MOBA_BLOCK = 256
MOBA_TOP_K = 3
ROPE_THETA = 10000.0
M_HEADS = 8
M_QK_DIM = 128
M_V_DIM = 256
M_CHUNK = 256
IGATE_CAP = 15.0
D_FF = 5632
EPS = 1e-6

D_MQ = M_HEADS * M_QK_DIM
D_MV = M_HEADS * M_V_DIM
M_MAIN = 2 * D_MQ + 2 * D_MV
SAMPLE_T_PAD = 8
PAGES_PER_BLOCK = MOBA_BLOCK // PAGE_SIZE

TM_PROMPT = 512
TM_PROJ = 1024
TF_FFN = 512
TN_PROJ = 512
VMEM_LIMIT = 56 * 1024 * 1024


def _cparams(sem):
    return pltpu.CompilerParams(dimension_semantics=sem, vmem_limit_bytes=VMEM_LIMIT)


def _dot(a, b):
    return jnp.dot(a, b, preferred_element_type=F32)


def _dot_nt(a, b, precision=None):
    return lax.dot_general(a, b, (((1,), (1,)), ((), ())), precision=precision,
                           preferred_element_type=F32)


def _dot_tn(a, b):
    return lax.dot_general(a, b, (((0,), (0,)), ((), ())), preferred_element_type=F32)


def _rms_norm_rows(x, g):
    ms = jnp.mean(x * x, axis=-1, keepdims=True)
    return x * lax.rsqrt(ms + EPS) * g


_FFN_SIDE_PAGES = 2
_CAST_TILE = 512


def _ffn_kernel(n_side, *refs):
    if n_side:
        refs = refs[1:]
    x_ref, g_ref, wg_ref, wu_ref, wd_ref = refs[:5]
    refs = refs[5:]
    if n_side:
        page_refs, (up_f32_ref, down_f32_ref) = refs[:n_side], refs[n_side:n_side + 2]
        o_ref, psum_ref, up_bf_ref, down_bf_ref, xn_ref, acc_ref = refs[n_side + 2:]
    else:
        o_ref, xn_ref, acc_ref = refs
    j = pl.program_id(1)

    @pl.when(j == 0)
    def _():
        xn_ref[...] = _rms_norm_rows(x_ref[...], g_ref[...]).astype(BF16)
        acc_ref[...] = jnp.zeros_like(acc_ref)

    xn = xn_ref[...]
    gate = _dot(xn, wg_ref[...])
    up = _dot(xn, wu_ref[...])
    h = (gate * jax.nn.sigmoid(gate) * up).astype(BF16)
    acc_ref[...] += _dot(h, wd_ref[...])

    if n_side:
        for u in range(n_side):
            psum_ref[u] = jnp.sum(page_refs[u][...], axis=0)
        up_bf_ref[...] = up_f32_ref[...].astype(BF16)
        down_bf_ref[...] = down_f32_ref[...].astype(BF16)

    @pl.when(j == pl.num_programs(1) - 1)
    def _():
        o_ref[...] = x_ref[...] + 0.5 * acc_ref[...]


def _ffn(x, g, w_up, w_down, layer, tm, side=None):
    t = x.shape[0]
    nj = D_FF // TF_FFN
    n_steps = (t // tm) * nj
    in_specs = [
        pl.BlockSpec((tm, D_MODEL), lambda i, j, *_: (i, 0)),
        pl.BlockSpec((None, 1, D_MODEL), lambda i, j, *_: (layer, 0, 0)),
        pl.BlockSpec((D_MODEL, TF_FFN), lambda i, j, *_: (0, j)),
        pl.BlockSpec((D_MODEL, TF_FFN), lambda i, j, *_: (0, j + nj)),
        pl.BlockSpec((TF_FFN, D_MODEL), lambda i, j, *_: (j, 0)),
    ]
    out_specs = pl.BlockSpec((tm, D_MODEL), lambda i, j, *_: (i, 0))
    out_shape = jax.ShapeDtypeStruct((t, D_MODEL), F32)
    scratch = [pltpu.VMEM((tm, D_MODEL), BF16), pltpu.VMEM((tm, D_MODEL), F32)]
    args = [x, g, w_up, w_up, w_down]
    if side is None:
        return pl.pallas_call(
            functools.partial(_ffn_kernel, 0),
            grid=(t // tm, nj), in_specs=in_specs, out_specs=out_specs, out_shape=out_shape,
            scratch_shapes=scratch, compiler_params=_cparams(("parallel", "arbitrary")), name="ffn",
        )(*args)

    pt_flat, cache_k, j_attn, first, count, next_up, next_down, next_layer = side
    n_side = _FFN_SIDE_PAGES
    n_groups = -(-count // n_side)
    up_cols, down_cols = 2 * D_FF // _CAST_TILE, D_MODEL // _CAST_TILE
    n_up_tiles = (D_MODEL // _CAST_TILE) * up_cols
    n_down_tiles = (D_FF // _CAST_TILE) * down_cols
    assert n_steps >= n_groups and n_steps >= n_up_tiles + n_down_tiles

    def page_spec(u):
        def index_map(i, j, pt):
            group = jnp.minimum(i * nj + j, n_groups - 1)
            return (j_attn, pt[first + jnp.minimum(group * n_side + u, count - 1)], 0, 0, 0)
        return pl.BlockSpec((None, None, PAGE_SIZE, N_HEADS, HEAD_DIM), index_map)

    def psum_map(i, j, pt):
        return (jnp.minimum(i * nj + j, n_groups - 1), 0, 0)

    def up_tile(i, j, pt):
        u = jnp.minimum(i * nj + j, n_up_tiles - 1)
        return (u // up_cols, u % up_cols)

    def down_tile(i, j, pt):
        d = jnp.clip(i * nj + j - n_up_tiles, 0, n_down_tiles - 1)
        return (d // down_cols, d % down_cols)

    tile = (_CAST_TILE, _CAST_TILE)
    grid_spec = pltpu.PrefetchScalarGridSpec(
        num_scalar_prefetch=1,
        grid=(t // tm, nj),
        in_specs=in_specs + [page_spec(u) for u in range(n_side)] + [
            pl.BlockSpec((None,) + tile, lambda i, j, pt: (next_layer,) + up_tile(i, j, pt)),
            pl.BlockSpec((None,) + tile, lambda i, j, pt: (next_layer,) + down_tile(i, j, pt)),
        ],
        out_specs=[out_specs, pl.BlockSpec((n_side, N_HEADS, HEAD_DIM), psum_map),
                   pl.BlockSpec(tile, up_tile), pl.BlockSpec(tile, down_tile)],
        scratch_shapes=scratch,
    )
    y, psum, up_bf, down_bf = pl.pallas_call(
        functools.partial(_ffn_kernel, n_side),
        grid_spec=grid_spec,
        out_shape=[out_shape, jax.ShapeDtypeStruct((n_groups * n_side, N_HEADS, HEAD_DIM), F32),
                   jax.ShapeDtypeStruct(next_up.shape[1:], BF16), jax.ShapeDtypeStruct(next_down.shape[1:], BF16)],
        compiler_params=_cparams(("arbitrary", "arbitrary")),
        name="ffn_side",
    )(pt_flat, *args, *([cache_k] * n_side), next_up, next_down)
    return y, psum[:count], up_bf, down_bf


_MXU_COLS = 256


def _dot_split(x, m):
    hi = x.astype(BF16)
    lo = (x - hi.astype(F32)).astype(BF16)
    return _dot(hi, m) + _dot(lo, m)


def _project_heads(xn_ref, w_ref, out_ref, g_ref, cos_ref, sin_ref):
    if g_ref is not None:
        r = lax.broadcasted_iota(jnp.int32, (_MXU_COLS, _MXU_COLS), 0)
        c = lax.broadcasted_iota(jnp.int32, (_MXU_COLS, _MXU_COLS), 1)
        same_head = (r // HEAD_DIM) == (c // HEAD_DIM)
        head_sum = same_head.astype(BF16)
        half_swap = (same_head & ((r % HEAD_DIM) == ((c + HEAD_DIM // 2) % HEAD_DIM))).astype(BF16)
        heads = _MXU_COLS // HEAD_DIM
        g = jnp.concatenate([g_ref[...]] * heads, axis=1)
    y_all = _dot(xn_ref[...], w_ref[...])
    if g_ref is None:
        out_ref[...] = y_all
        return
    for p in range(TN_PROJ // _MXU_COLS):
        y = y_all[:, p * _MXU_COLS:(p + 1) * _MXU_COLS]
        ms = _dot_split(y * y, head_sum) * (1.0 / HEAD_DIM)
        yn = y * lax.rsqrt(ms + EPS) * g
        rot = _dot_split(yn, half_swap)
        for hh in range(heads):
            lo = p * _MXU_COLS + hh * HEAD_DIM
            sl = slice(hh * HEAD_DIM, (hh + 1) * HEAD_DIM)
            out_ref[:, lo:lo + HEAD_DIM] = yn[:, sl] * cos_ref[...] + rot[:, sl] * sin_ref[...]


def _qkv_kernel(x_ref, g_ref, w_ref, gq_ref, gk_ref, cos_ref, sin_ref, q_ref, k_ref, v_ref, xn_ref):
    j = pl.program_id(1)
    n_q = D_MODEL // TN_PROJ

    @pl.when(j == 0)
    def _():
        xn_ref[...] = _rms_norm_rows(x_ref[...], g_ref[...]).astype(BF16)

    @pl.when(j < n_q)
    def _():
        _project_heads(xn_ref, w_ref, q_ref, gq_ref, cos_ref, sin_ref)

    @pl.when((j >= n_q) & (j < 2 * n_q))
    def _():
        _project_heads(xn_ref, w_ref, k_ref, gk_ref, cos_ref, sin_ref)

    @pl.when(j >= 2 * n_q)
    def _():
        _project_heads(xn_ref, w_ref, v_ref, None, None, None)


def _qkv(x, g_mix, w_qkv, g_q, g_k, cos, sin, layer, j_attn, tm):
    t = x.shape[0]
    n_q = D_MODEL // TN_PROJ
    n_pos_blocks = cos.shape[0] // tm
    out_sd = jax.ShapeDtypeStruct((t, D_MODEL), F32)
    return pl.pallas_call(
        _qkv_kernel,
        grid=(t // tm, 3 * n_q),
        in_specs=[
            pl.BlockSpec((tm, D_MODEL), lambda i, j: (i, 0)),
            pl.BlockSpec((None, 1, D_MODEL), lambda i, j: (layer, 0, 0)),
            pl.BlockSpec((None, D_MODEL, TN_PROJ), lambda i, j: (j_attn, 0, j)),
            pl.BlockSpec((None, 1, HEAD_DIM), lambda i, j: (j_attn, 0, 0)),
            pl.BlockSpec((None, 1, HEAD_DIM), lambda i, j: (j_attn, 0, 0)),
            pl.BlockSpec((tm, HEAD_DIM), lambda i, j: (i % n_pos_blocks, 0)),
            pl.BlockSpec((tm, HEAD_DIM), lambda i, j: (i % n_pos_blocks, 0)),
        ],
        out_specs=[
            pl.BlockSpec((tm, TN_PROJ), lambda i, j: (i, jnp.minimum(j, n_q - 1))),
            pl.BlockSpec((tm, TN_PROJ), lambda i, j: (i, jnp.clip(j - n_q, 0, n_q - 1))),
            pl.BlockSpec((tm, TN_PROJ), lambda i, j: (i, jnp.clip(j - 2 * n_q, 0, n_q - 1))),
        ],
        out_shape=[out_sd, out_sd, out_sd],
        scratch_shapes=[pltpu.VMEM((tm, D_MODEL), BF16)],
        compiler_params=_cparams(("parallel", "arbitrary")),
        name="qkv",
    )(x, g_mix, w_qkv, g_q, g_k, cos, sin)


_MASK_BIAS = -(2.0 ** 100)


def _moba_prompt_kernel(q_ref, k_ref, v_ref, o_ref, qa_ref, ka_ref, vb_ref):
    seq = k_ref.shape[0]
    nb = seq // MOBA_BLOCK
    q = q_ref[...]
    k = k_ref[...]

    kmean = jnp.mean(k.reshape(nb, MOBA_BLOCK, HEAD_DIM), axis=1)
    gate_t = _dot_nt(kmean, q, precision=HIGHEST)
    blk = lax.broadcasted_iota(jnp.int32, gate_t.shape, 0)
    qblk = lax.broadcasted_iota(jnp.int32, gate_t.shape, 1) // MOBA_BLOCK
    cnt = jnp.zeros(gate_t.shape, jnp.int32)
    for kk in range(nb):
        gk = gate_t[kk:kk + 1, :]
        beats = (gk > gate_t) | ((gk == gate_t) & (kk < blk))
        cnt = cnt + jnp.where(beats & (kk < qblk), 1, 0)
    allowed = ((cnt < MOBA_TOP_K) & (blk < qblk)) | (blk == qblk)
    bias_t = jnp.where(allowed, 0.0, _MASK_BIAS)
    eye = (lax.broadcasted_iota(jnp.int32, (nb, HEAD_DIM), 0)
           == lax.broadcasted_iota(jnp.int32, (nb, HEAD_DIM), 1)).astype(F32)
    bias_q = _dot_tn(bias_t, eye)

    key_blk = lax.broadcasted_iota(jnp.int32, (seq, HEAD_DIM), 0) // MOBA_BLOCK
    key_onehot = key_blk == lax.broadcasted_iota(jnp.int32, (seq, HEAD_DIM), 1)
    qa_ref[:, 0:HEAD_DIM] = q.astype(BF16)
    qa_ref[:, HEAD_DIM:2 * HEAD_DIM] = bias_q.astype(BF16)
    ka_ref[:, 0:HEAD_DIM] = k.astype(BF16)
    ka_ref[:, HEAD_DIM:2 * HEAD_DIM] = key_onehot.astype(BF16)
    vb_ref[...] = v_ref[...].astype(BF16)

    row = lax.broadcasted_iota(jnp.int32, (MOBA_BLOCK, MOBA_BLOCK), 0)
    col = lax.broadcasted_iota(jnp.int32, (MOBA_BLOCK, MOBA_BLOCK), 1)
    causal = col <= row
    exp2_scale = (HEAD_DIM ** -0.5) * 1.4426950408889634
    for i in range(nb):
        lo, hi = i * MOBA_BLOCK, (i + 1) * MOBA_BLOCK
        s = _dot_nt(qa_ref[lo:hi, :], ka_ref[0:hi, :])
        own = jnp.where(causal, s[:, lo:hi], _MASK_BIAS)
        s = own if i == 0 else jnp.concatenate([s[:, 0:lo], own], axis=1)
        m = jnp.max(s, axis=-1, keepdims=True)
        p = jnp.exp2((s - m) * exp2_scale)
        l = jnp.sum(p, axis=-1, keepdims=True)
        o_ref[lo:hi, :] = (_dot(p.astype(BF16), vb_ref[0:hi, :]) / l).astype(o_ref.dtype)


def _moba_prompt(q, k, v, batch, seq):
    blk = pl.BlockSpec((seq, HEAD_DIM), lambda b, h: (b, h))
    return pl.pallas_call(
        _moba_prompt_kernel,
        grid=(batch, N_HEADS),
        in_specs=[blk, blk, blk],
        out_specs=blk,
        out_shape=jax.ShapeDtypeStruct((batch * seq, N_HEADS * HEAD_DIM), BF16),
        scratch_shapes=[
            pltpu.VMEM((seq, 2 * HEAD_DIM), BF16),
            pltpu.VMEM((seq, 2 * HEAD_DIM), BF16),
            pltpu.VMEM((seq, HEAD_DIM), BF16),
        ],
        compiler_params=_cparams(("parallel", "parallel")),
        name="moba_prompt",
    )(q, k, v)


def _proj_res_kernel(a_ref, w_ref, x_ref, o_ref):
    o_ref[...] = x_ref[...] + _dot(a_ref[...].astype(BF16), w_ref[...])


def _proj_res(a, w, x, j_w, tm):
    t = x.shape[0]
    return pl.pallas_call(
        _proj_res_kernel,
        grid=(t // tm,),
        in_specs=[
            pl.BlockSpec((tm, D_MODEL), lambda i: (i, 0)),
            pl.BlockSpec((None, D_MODEL, D_MODEL), lambda i: (j_w, 0, 0)),
            pl.BlockSpec((tm, D_MODEL), lambda i: (i, 0)),
        ],
        out_specs=pl.BlockSpec((tm, D_MODEL), lambda i: (i, 0)),
        out_shape=jax.ShapeDtypeStruct((t, D_MODEL), F32),
        compiler_params=_cparams(("parallel",)),
        name="proj_res",
    )(a, w, x)


def _mlstm_in_kernel(x_ref, g_ref, w_ref, wg_ref, y_ref, gates_ref, xn_ref):
    j = pl.program_id(1)

    @pl.when(j == 0)
    def _():
        xn = _rms_norm_rows(x_ref[...], g_ref[...]).astype(BF16)
        xn_ref[...] = xn
        gates_ref[...] = _dot(xn, wg_ref[...])

    y_ref[...] = _dot(xn_ref[...], w_ref[...])


def _mlstm_in(x, g_mix, w_main, w_gates, layer, j_m, tm):
    t = x.shape[0]
    return pl.pallas_call(
        _mlstm_in_kernel,
        grid=(t // tm, M_MAIN // TN_PROJ),
        in_specs=[
            pl.BlockSpec((tm, D_MODEL), lambda i, j: (i, 0)),
            pl.BlockSpec((None, 1, D_MODEL), lambda i, j: (layer, 0, 0)),
            pl.BlockSpec((None, D_MODEL, TN_PROJ), lambda i, j: (j_m, 0, j)),
            pl.BlockSpec((None, D_MODEL, 128), lambda i, j: (j_m, 0, 0)),
        ],
        out_specs=[
            pl.BlockSpec((tm, TN_PROJ), lambda i, j: (i, j)),
            pl.BlockSpec((tm, 128), lambda i, j: (i, 0)),
        ],
        out_shape=[jax.ShapeDtypeStruct((t, M_MAIN), F32), jax.ShapeDtypeStruct((t, 128), F32)],
        scratch_shapes=[pltpu.VMEM((tm, D_MODEL), BF16)],
        compiler_params=_cparams(("parallel", "arbitrary")),
        name="mlstm_in",
    )(x, g_mix, w_main, w_gates)


def _mlstm_chunk_kernel(l_valid, zero_init, *refs):
    if zero_init:
        main_ref, gates_ref, bias_ref, gh_ref = refs[:4]
        rest = refs[4:]
    else:
        main_ref, gates_ref, bias_ref, gh_ref, c0_ref, n0_ref, m0_ref = refs[:7]
        rest = refs[7:]
    hn_ref, cout_ref, nout_ref, mout_ref, c_s, n_s, m_s = rest
    c = pl.program_id(1)
    L = main_ref.shape[0]
    scale_k = M_QK_DIM ** -0.5

    @pl.when(c == 0)
    def _():
        if zero_init:
            c_s[...] = jnp.zeros_like(c_s)
            n_s[...] = jnp.zeros_like(n_s)
            m_s[...] = jnp.zeros_like(m_s)
        else:
            c_s[...] = c0_ref[...]
            n_s[...] = n0_ref[...]
            for h in range(M_HEADS):
                m_s[h] = jnp.broadcast_to(m0_ref[h], (8, 128))

    pre = gates_ref[...] + bias_ref[...]
    lane = lax.broadcasted_iota(jnp.int32, pre.shape, 1)
    act = jnp.where(lane < M_HEADS, IGATE_CAP * jnp.tanh(pre / IGATE_CAP), jax.nn.log_sigmoid(pre))
    r = lax.broadcasted_iota(jnp.int32, (L, L), 0)
    cc = lax.broadcasted_iota(jnp.int32, (L, L), 1)
    causal = cc <= r
    bcols = jnp.dot(causal.astype(F32), act, precision=HIGHEST, preferred_element_type=F32)
    act_t = act.T
    b_t = bcols.T
    rows = lax.broadcasted_iota(jnp.int32, (L, 1), 0)

    for h in range(M_HEADS):
        q = main_ref[:, h * M_QK_DIM:(h + 1) * M_QK_DIM]
        k = main_ref[:, D_MQ + h * M_QK_DIM:D_MQ + (h + 1) * M_QK_DIM]
        v = main_ref[:, 2 * D_MQ + h * M_V_DIM:2 * D_MQ + (h + 1) * M_V_DIM]
        o = main_ref[:, 2 * D_MQ + D_MV + h * M_V_DIM:2 * D_MQ + D_MV + (h + 1) * M_V_DIM]
        b_col = bcols[:, M_HEADS + h:M_HEADS + h + 1]
        b_row = b_t[M_HEADS + h:M_HEADS + h + 1, :]
        li_col = act[:, h:h + 1]
        li_row = act_t[h:h + 1, :]
        c0 = c_s[h]
        n0 = n_s[h]
        m0 = m_s[h][0:1, 0:1]

        dlog = jnp.where(causal, b_col - b_row + li_row, -jnp.inf)
        inter = b_col + m0
        m = jnp.maximum(inter, jnp.max(dlog, axis=-1, keepdims=True))
        w = jnp.exp(dlog - m)
        a = jnp.exp(inter - m)
        qb = q.astype(BF16)
        kb = k.astype(BF16)
        vb = v.astype(BF16)
        s = _dot_nt(qb, kb) * scale_k * w
        num = _dot(s.astype(BF16), vb) + a * _dot(qb, c0.astype(BF16))
        den = jnp.sum(s, axis=-1, keepdims=True) + a * jnp.sum(q * n0, axis=-1, keepdims=True)
        hh = num / jnp.maximum(jnp.abs(den), jnp.exp(-m))
        hn = hh * lax.rsqrt(jnp.mean(hh * hh, axis=-1, keepdims=True) + EPS)
        hn = hn * gh_ref[:, h * M_V_DIM:(h + 1) * M_V_DIM] * jax.nn.sigmoid(o)
        hn_ref[:, h * M_V_DIM:(h + 1) * M_V_DIM] = hn.astype(hn_ref.dtype)

        m_end = m[l_valid - 1:l_valid]
        a_end = a[l_valid - 1:l_valid]
        b_last = b_col[l_valid - 1:l_valid]
        wk = jnp.exp(b_last - b_col + li_col - m_end)
        if l_valid < L:
            wk = jnp.where(rows < l_valid, wk, 0.0)
        kw = k * (scale_k * wk)
        c_s[h] = a_end * c0 + _dot_tn(kw.astype(BF16), vb)
        n_s[h] = a_end * n0 + jnp.sum(kw, axis=0, keepdims=True)
        m_s[h] = jnp.broadcast_to(m_end, (8, 128))

    @pl.when(c == pl.num_programs(1) - 1)
    def _():
        cout_ref[...] = c_s[...]
        nout_ref[...] = n_s[...]
        mout_ref[...] = m_s[...]


def _mlstm_chunk(main, gates, bias, g_h, j_m, n_seq, n_chunks, chunk, l_valid, state=None):
    zero_init = state is None
    t = main.shape[0]
    in_specs = [
        pl.BlockSpec((chunk, M_MAIN), lambda n, c: (n * n_chunks + c, 0)),
        pl.BlockSpec((chunk, 128), lambda n, c: (n * n_chunks + c, 0)),
        pl.BlockSpec((None, 1, 128), lambda n, c: (j_m, 0, 0)),
        pl.BlockSpec((None, 1, D_MV), lambda n, c: (j_m, 0, 0)),
    ]
    args = [main, gates, bias, g_h]
    if not zero_init:
        in_specs += [
            pl.BlockSpec((None, None, M_HEADS, M_QK_DIM, M_V_DIM), lambda n, c: (j_m, n, 0, 0, 0)),
            pl.BlockSpec((None, None, M_HEADS, 1, M_QK_DIM), lambda n, c: (j_m, n, 0, 0, 0)),
            pl.BlockSpec((None, None, M_HEADS, 1, 1), lambda n, c: (j_m, n, 0, 0, 0)),
        ]
        args += list(state)
    hn_dtype = BF16 if chunk % 16 == 0 else F32
    return pl.pallas_call(
        functools.partial(_mlstm_chunk_kernel, l_valid, zero_init),
        grid=(n_seq, n_chunks),
        in_specs=in_specs,
        out_specs=[
            pl.BlockSpec((chunk, D_MV), lambda n, c: (n * n_chunks + c, 0)),
            pl.BlockSpec((None, M_HEADS, M_QK_DIM, M_V_DIM), lambda n, c: (n, 0, 0, 0)),
            pl.BlockSpec((None, M_HEADS, 1, M_QK_DIM), lambda n, c: (n, 0, 0, 0)),
            pl.BlockSpec((None, M_HEADS, 8, 128), lambda n, c: (n, 0, 0, 0)),
        ],
        out_shape=[
            jax.ShapeDtypeStruct((t, D_MV), hn_dtype),
            jax.ShapeDtypeStruct((n_seq, M_HEADS, M_QK_DIM, M_V_DIM), F32),
            jax.ShapeDtypeStruct((n_seq, M_HEADS, 1, M_QK_DIM), F32),
            jax.ShapeDtypeStruct((n_seq, M_HEADS, 8, 128), F32),
        ],
        scratch_shapes=[
            pltpu.VMEM((M_HEADS, M_QK_DIM, M_V_DIM), F32),
            pltpu.VMEM((M_HEADS, 1, M_QK_DIM), F32),
            pltpu.VMEM((M_HEADS, 8, 128), F32),
        ],
        compiler_params=_cparams(("parallel", "arbitrary")),
        name="mlstm_chunk",
    )(*args)


def _sample_select_kernel(q_ref, psum_ref, sel_ref, kmean_ref):
    c_past = kmean_ref.shape[1]

    for c in range(c_past):
        tot = psum_ref[c * PAGES_PER_BLOCK]
        for pp in range(1, PAGES_PER_BLOCK):
            tot = tot + psum_ref[c * PAGES_PER_BLOCK + pp]
        tot = tot * (1.0 / MOBA_BLOCK)
        for h in range(N_HEADS):
            kmean_ref[h, c:c + 1, :] = tot[h:h + 1, :]

    out_lane = lax.broadcasted_iota(jnp.int32, sel_ref.shape, 1)
    out = jnp.zeros(sel_ref.shape, jnp.int32)
    for h in range(N_HEADS):
        g = _dot_nt(q_ref[:, h * HEAD_DIM:(h + 1) * HEAD_DIM], kmean_ref[h], precision=HIGHEST)
        lane = lax.broadcasted_iota(jnp.int32, g.shape, 1)
        for kk in range(MOBA_TOP_K):
            mx = jnp.max(g, axis=-1, keepdims=True)
            idx = jnp.min(jnp.where(g == mx, lane, c_past), axis=-1, keepdims=True)
            out = jnp.where(out_lane == h * MOBA_TOP_K + kk, idx, out)
            g = jnp.where(lane == idx, -jnp.inf, g)
    sel_ref[...] = out


def _sample_select(psum, q, n_db):
    n_pages = psum.shape[1]
    c_past = n_pages // PAGES_PER_BLOCK
    return pl.pallas_call(
        _sample_select_kernel,
        grid=(n_db,),
        in_specs=[
            pl.BlockSpec((SAMPLE_T_PAD, N_HEADS * HEAD_DIM), lambda db: (db, 0)),
            pl.BlockSpec((None, n_pages, N_HEADS, HEAD_DIM), lambda db: (db, 0, 0, 0)),
        ],
        out_specs=pl.BlockSpec((None, SAMPLE_T_PAD, 128), lambda db: (db, 0, 0)),
        out_shape=jax.ShapeDtypeStruct((n_db, SAMPLE_T_PAD, 128), jnp.int32),
        scratch_shapes=[pltpu.VMEM((N_HEADS, c_past, HEAD_DIM), F32)],
        compiler_params=_cparams(("parallel",)),
        name="sample_select",
    )(q, psum)


_N_SEL_PAGES = MOBA_TOP_K * PAGES_PER_BLOCK


def _sample_attn_kernel(t_valid, j_attn, pt_ref, sel_ref, q_ref, kn_ref, vn_ref, ck_ref, cv_ref, o_ref,
                        kbuf, vbuf, sem):
    n_heads = pl.num_programs(1)
    step = pl.program_id(0) * n_heads + pl.program_id(1)
    n_steps = pl.num_programs(0) * n_heads
    slot = step % 2
    scale = HEAD_DIM ** -0.5

    def copies(step_, slot_):
        db_ = step_ // n_heads
        h_ = step_ % n_heads
        out = []
        for t in range(t_valid):
            for kk in range(MOBA_TOP_K):
                blk = sel_ref[db_ * (SAMPLE_T_PAD * 128) + t * 128 + h_ * MOBA_TOP_K + kk]
                for pp in range(PAGES_PER_BLOCK):
                    page = pt_ref[db_, blk * PAGES_PER_BLOCK + pp]
                    i = (t * MOBA_TOP_K + kk) * PAGES_PER_BLOCK + pp
                    out.append(pltpu.make_async_copy(ck_ref.at[j_attn, page, :, h_, :], kbuf.at[slot_, i],
                                                     sem.at[0, slot_]))
                    out.append(pltpu.make_async_copy(cv_ref.at[j_attn, page, :, h_, :], vbuf.at[slot_, i],
                                                     sem.at[1, slot_]))
        return out

    @pl.when(step == 0)
    def _():
        for c in copies(step, slot):
            c.start()

    @pl.when(step + 1 < n_steps)
    def _():
        for c in copies(step + 1, 1 - slot):
            c.start()

    for c in copies(step, slot):
        c.wait()

    kn = kn_ref[...].astype(BF16)
    vn = vn_ref[...].astype(BF16)
    lane = lax.broadcasted_iota(jnp.int32, (SAMPLE_T_PAD, SAMPLE_T_PAD), 1)
    rows = []
    for t in range(t_valid):
        q = jnp.broadcast_to(q_ref[t:t + 1, :], (SAMPLE_T_PAD, HEAD_DIM)).astype(BF16)
        ks = kbuf[slot, t * _N_SEL_PAGES:(t + 1) * _N_SEL_PAGES].reshape(_N_SEL_PAGES * PAGE_SIZE, HEAD_DIM)
        vs = vbuf[slot, t * _N_SEL_PAGES:(t + 1) * _N_SEL_PAGES].reshape(_N_SEL_PAGES * PAGE_SIZE, HEAD_DIM)
        s_sel = _dot_nt(q, ks.astype(BF16)) * scale
        s_own = jnp.where(lane <= t, _dot_nt(q, kn) * scale, -jnp.inf)
        m = jnp.maximum(jnp.max(s_sel, axis=-1, keepdims=True), jnp.max(s_own, axis=-1, keepdims=True))
        p_sel = jnp.exp(s_sel - m)
        p_own = jnp.exp(s_own - m)
        l = jnp.sum(p_sel, axis=-1, keepdims=True) + jnp.sum(p_own, axis=-1, keepdims=True)
        out = (_dot(p_sel.astype(BF16), vs.astype(BF16)) + _dot(p_own.astype(BF16), vn)) / l
        rows.append(out[0:1, :])
    rows.append(jnp.zeros((SAMPLE_T_PAD - t_valid, HEAD_DIM), F32))
    o_ref[...] = jnp.concatenate(rows, axis=0)


def _sample_attn(page_table, sel_flat, q, k_new, v_new, cache_k, cache_v, j_attn, n_db, t_valid):
    row_spec = pl.BlockSpec((SAMPLE_T_PAD, HEAD_DIM), lambda db, h, pt, sel: (db, h))
    hbm_spec = pl.BlockSpec(memory_space=pl.ANY)
    n_slices = t_valid * _N_SEL_PAGES
    grid_spec = pltpu.PrefetchScalarGridSpec(
        num_scalar_prefetch=2,
        grid=(n_db, N_HEADS),
        in_specs=[row_spec, row_spec, row_spec, hbm_spec, hbm_spec],
        out_specs=row_spec,
        scratch_shapes=[
            pltpu.VMEM((2, n_slices, PAGE_SIZE, HEAD_DIM), F32),
            pltpu.VMEM((2, n_slices, PAGE_SIZE, HEAD_DIM), F32),
            pltpu.SemaphoreType.DMA((2, 2)),
        ],
    )
    return pl.pallas_call(
        functools.partial(_sample_attn_kernel, t_valid, j_attn),
        grid_spec=grid_spec,
        out_shape=jax.ShapeDtypeStruct((n_db * SAMPLE_T_PAD, N_HEADS * HEAD_DIM), F32),
        compiler_params=_cparams(("arbitrary", "arbitrary")),
        name="sample_attn",
    )(page_table, sel_flat, q, k_new, v_new, cache_k, cache_v)


def _rope_tables(pos):
    half = HEAD_DIM // 2
    inv = ROPE_THETA ** (-jnp.arange(half, dtype=F32) / half)
    ang = pos.astype(F32)[:, None] * inv[None, :]
    cos = jnp.cos(ang)
    sin = jnp.sin(ang)
    return jnp.concatenate([cos, cos], axis=-1), jnp.concatenate([-sin, sin], axis=-1)


def kernel(x_prompt, x_sample, cache_k, cache_v, state_C, state_n, state_m, page_table, g_ffn1, w_ffn1_up, w_ffn1_down, g_mix, attn_w_qkv, attn_g_q, attn_g_k, attn_w_o, mlstm_w_in, mlstm_b_if, mlstm_g_h, mlstm_w_out, g_ffn2, w_ffn2_up, w_ffn2_down):
    b_, s_, d = x_prompt.shape
    db, t_s, _ = x_sample.shape
    assert d == D_MODEL and s_ % MOBA_BLOCK == 0 and s_ % TM_PROJ == 0 and s_ % M_CHUNK == 0 and t_s <= SAMPLE_T_PAD
    assert PAST_LEN % MOBA_BLOCK == 0 and page_table.shape[1] * PAGE_SIZE == PAST_LEN
    n_mlstm = mlstm_w_in.shape[0]

    ffn_w = [(w_ffn1_up[0].astype(BF16), w_ffn1_down[0].astype(BF16))]
    wqkv, wo = attn_w_qkv.astype(BF16), attn_w_o.astype(BF16)
    w_in_main = mlstm_w_in.astype(BF16)
    w_in_gates = jnp.pad(mlstm_w_in[:, :, M_MAIN:], ((0, 0), (0, 0), (0, 128 - 2 * M_HEADS))).astype(BF16)
    w_out = mlstm_w_out.astype(BF16)
    g1 = g_ffn1.reshape(DEPTH, 1, d)
    g2 = g_ffn2.reshape(DEPTH, 1, d)
    gm = g_mix.reshape(DEPTH, 1, d)
    gq = attn_g_q.reshape(-1, 1, HEAD_DIM)
    gk = attn_g_k.reshape(-1, 1, HEAD_DIM)
    b_if = jnp.pad(mlstm_b_if, ((0, 0), (0, 128 - 2 * M_HEADS))).reshape(n_mlstm, 1, 128)
    g_h = mlstm_g_h.reshape(n_mlstm, 1, D_MV)

    cos_p, sin_p = _rope_tables(jnp.arange(s_))
    cos_s, sin_s = _rope_tables(PAST_LEN + (jnp.arange(db * SAMPLE_T_PAD) % SAMPLE_T_PAD))

    st_n = state_n.reshape(n_mlstm, db, M_HEADS, 1, M_QK_DIM)
    st_m = state_m.reshape(n_mlstm, db, M_HEADS, 1, 1)

    xp = x_prompt.reshape(b_ * s_, d)
    xs = jnp.pad(x_sample, ((0, 0), (0, SAMPLE_T_PAD - t_s), (0, 0))).reshape(db * SAMPLE_T_PAD, d)
    tm_s = db * SAMPLE_T_PAD

    n_attn = cache_k.shape[0]
    pt_flat = page_table.reshape(-1)
    pages_per_layer = pt_flat.shape[0]
    pages_per_call = n_attn * pages_per_layer // (2 * DEPTH)
    calls_per_layer = pages_per_layer // pages_per_call
    assert pages_per_call * 2 * DEPTH == n_attn * pages_per_layer and calls_per_layer * pages_per_call == pages_per_layer
    psums = [[] for _ in range(n_attn)]

    def ffn_prompt(x, i, second):
        call = 2 * i + second
        j_attn = call // calls_per_layer
        if second:
            nxt = (w_ffn1_up, w_ffn1_down, (i + 1) % DEPTH)
        else:
            nxt = (w_ffn2_up, w_ffn2_down, i)
        side = (pt_flat, cache_k, j_attn, (call % calls_per_layer) * pages_per_call, pages_per_call) + nxt
        y, ps, up_bf, down_bf = _ffn(x, g2 if second else g1, *ffn_w[call], i, TM_PROMPT, side=side)
        psums[j_attn].append(ps)
        ffn_w.append((up_bf, down_bf))
        return y

    kp_l, vp_l, ks_l, vs_l = [], [], [], []
    cp_l, np_l, mp_l, cs_l, ns_l, ms_l = [], [], [], [], [], []
    for i in range(DEPTH):
        xp = ffn_prompt(xp, i, 0)
        j = i // 2
        if i % 2 == 0:
            qp, kp, vp = _qkv(xp, gm, wqkv, gq, gk, cos_p, sin_p, i, j, TM_PROJ)
            op = _moba_prompt(qp, kp, vp, b_, s_)
            xp = _proj_res(op, wo, xp, j, TM_PROMPT)
            kp_l.append(kp.reshape(b_, s_ // PAGE_SIZE, PAGE_SIZE, N_HEADS, HEAD_DIM))
            vp_l.append(vp.reshape(b_, s_ // PAGE_SIZE, PAGE_SIZE, N_HEADS, HEAD_DIM))
        else:
            main, gates = _mlstm_in(xp, gm, w_in_main, w_in_gates, i, j, TM_PROJ)
            hn, c_p, n_p, m_p = _mlstm_chunk(main, gates, b_if, g_h, j, b_, s_ // M_CHUNK, M_CHUNK, M_CHUNK)
            xp = _proj_res(hn, w_out, xp, j, TM_PROMPT)
            cp_l.append(c_p)
            np_l.append(n_p[:, :, 0, :])
            mp_l.append(m_p[:, :, 0, 0])
        xp = ffn_prompt(xp, i, 1)

    psum_l = [jnp.concatenate(p, axis=0).reshape(db, -1, N_HEADS, HEAD_DIM) for p in psums]
    for i in range(DEPTH):
        xs = _ffn(xs, g1, *ffn_w[2 * i], i, tm_s)
        j = i // 2
        if i % 2 == 0:
            qs, ks, vs = _qkv(xs, gm, wqkv, gq, gk, cos_s, sin_s, i, j, tm_s)
            sel = _sample_select(psum_l[j], qs, db)
            os_ = _sample_attn(page_table, sel.reshape(-1), qs, ks, vs, cache_k, cache_v, j, db, t_s)
            xs = _proj_res(os_, wo, xs, j, tm_s)
            ks_l.append(ks.reshape(db, SAMPLE_T_PAD, N_HEADS, HEAD_DIM)[:, :t_s])
            vs_l.append(vs.reshape(db, SAMPLE_T_PAD, N_HEADS, HEAD_DIM)[:, :t_s])
        else:
            main, gates = _mlstm_in(xs, gm, w_in_main, w_in_gates, i, j, tm_s)
            hn, c_s, n_s, m_s = _mlstm_chunk(main, gates, b_if, g_h, j, db, 1, SAMPLE_T_PAD, t_s,
                                             state=(state_C, st_n, st_m))
            xs = _proj_res(hn, w_out, xs, j, tm_s)
            cs_l.append(c_s)
            ns_l.append(n_s[:, :, 0, :])
            ms_l.append(m_s[:, :, 0, 0])
        xs = _ffn(xs, g2, *ffn_w[2 * i + 1], i, tm_s)

    y_prompt = xp.reshape(b_, s_, d)
    y_sample = xs.reshape(db, SAMPLE_T_PAD, d)[:, :t_s]
    return (y_prompt, y_sample, jnp.stack(kp_l), jnp.stack(vp_l), jnp.stack(ks_l), jnp.stack(vs_l),
            jnp.stack(cp_l), jnp.stack(np_l), jnp.stack(mp_l), jnp.stack(cs_l), jnp.stack(ns_l), jnp.stack(ms_l))
```

```python
import functools

import jax
import jax.numpy as jnp
from jax import lax
from jax.experimental import pallas as pl
from jax.experimental.pallas import tpu as pltpu

F32 = jnp.float32
BF16 = jnp.bfloat16
HIGHEST = lax.Precision.HIGHEST

D_MODEL = 2048
DEPTH = 4
PAST_LEN = 16384
PAGE_SIZE = 128
N_HEADS = 16
HEAD_DIM = 128
MOBA_BLOCK = 256
MOBA_TOP_K = 3
ROPE_THETA = 10000.0
M_HEADS = 8
M_QK_DIM = 128
M_V_DIM = 256
M_CHUNK = 256
IGATE_CAP = 15.0
D_FF = 5632
EPS = 1e-6

D_MQ = M_HEADS * M_QK_DIM
D_MV = M_HEADS * M_V_DIM
M_MAIN = 2 * D_MQ + 2 * D_MV
SAMPLE_T_PAD = 8
PAGES_PER_BLOCK = MOBA_BLOCK // PAGE_SIZE

TM_PROMPT = 512
TM_PROJ = 1024
TF_FFN = 512
TN_PROJ = 512
VMEM_LIMIT = 56 * 1024 * 1024


def _cparams(sem):
    return pltpu.CompilerParams(dimension_semantics=sem, vmem_limit_bytes=VMEM_LIMIT)


def _dot(a, b):
    return jnp.dot(a, b, preferred_element_type=F32)


def _dot_nt(a, b, precision=None):
    return lax.dot_general(a, b, (((1,), (1,)), ((), ())), precision=precision,
                           preferred_element_type=F32)


def _dot_tn(a, b):
    return lax.dot_general(a, b, (((0,), (0,)), ((), ())), preferred_element_type=F32)


def _rms_norm_rows(x, g):
    ms = jnp.mean(x * x, axis=-1, keepdims=True)
    return x * lax.rsqrt(ms + EPS) * g


_FFN_SIDE_PAGES = 2
_CAST_TILE = 512


def _ffn_kernel(n_side, *refs):
    if n_side:
        refs = refs[1:]
    x_ref, g_ref, wg_ref, wu_ref, wd_ref = refs[:5]
    refs = refs[5:]
    if n_side:
        page_refs, (up_f32_ref, down_f32_ref) = refs[:n_side], refs[n_side:n_side + 2]
        o_ref, psum_ref, up_bf_ref, down_bf_ref, xn_ref, acc_ref = refs[n_side + 2:]
    else:
        o_ref, xn_ref, acc_ref = refs
    j = pl.program_id(1)

    @pl.when(j == 0)
    def _():
        xn_ref[...] = _rms_norm_rows(x_ref[...], g_ref[...]).astype(BF16)
        acc_ref[...] = jnp.zeros_like(acc_ref)

    xn = xn_ref[...]
    gate = _dot(xn, wg_ref[...])
    up = _dot(xn, wu_ref[...])
    h = (gate * jax.nn.sigmoid(gate) * up).astype(BF16)
    acc_ref[...] += _dot(h, wd_ref[...])

    if n_side:
        for u in range(n_side):
            psum_ref[u] = jnp.sum(page_refs[u][...], axis=0)
        up_bf_ref[...] = up_f32_ref[...].astype(BF16)
        down_bf_ref[...] = down_f32_ref[...].astype(BF16)

    @pl.when(j == pl.num_programs(1) - 1)
    def _():
        o_ref[...] = x_ref[...] + 0.5 * acc_ref[...]


def _ffn(x, g, w_up, w_down, layer, tm, side=None):
    t = x.shape[0]
    nj = D_FF // TF_FFN
    n_steps = (t // tm) * nj
    in_specs = [
        pl.BlockSpec((tm, D_MODEL), lambda i, j, *_: (i, 0)),
        pl.BlockSpec((None, 1, D_MODEL), lambda i, j, *_: (layer, 0, 0)),
        pl.BlockSpec((D_MODEL, TF_FFN), lambda i, j, *_: (0, j)),
        pl.BlockSpec((D_MODEL, TF_FFN), lambda i, j, *_: (0, j + nj)),
        pl.BlockSpec((TF_FFN, D_MODEL), lambda i, j, *_: (j, 0)),
    ]
    out_specs = pl.BlockSpec((tm, D_MODEL), lambda i, j, *_: (i, 0))
    out_shape = jax.ShapeDtypeStruct((t, D_MODEL), F32)
    scratch = [pltpu.VMEM((tm, D_MODEL), BF16), pltpu.VMEM((tm, D_MODEL), F32)]
    args = [x, g, w_up, w_up, w_down]
    if side is None:
        return pl.pallas_call(
            functools.partial(_ffn_kernel, 0),
            grid=(t // tm, nj), in_specs=in_specs, out_specs=out_specs, out_shape=out_shape,
            scratch_shapes=scratch, compiler_params=_cparams(("parallel", "arbitrary")), name="ffn",
        )(*args)

    pt_flat, cache_k, j_attn, first, count, next_up, next_down, next_layer = side
    n_side = _FFN_SIDE_PAGES
    n_groups = -(-count // n_side)
    up_cols, down_cols = 2 * D_FF // _CAST_TILE, D_MODEL // _CAST_TILE
    n_up_tiles = (D_MODEL // _CAST_TILE) * up_cols
    n_down_tiles = (D_FF // _CAST_TILE) * down_cols
    assert n_steps >= n_groups and n_steps >= n_up_tiles + n_down_tiles

    def page_spec(u):
        def index_map(i, j, pt):
            group = jnp.minimum(i * nj + j, n_groups - 1)
            return (j_attn, pt[first + jnp.minimum(group * n_side + u, count - 1)], 0, 0, 0)
        return pl.BlockSpec((None, None, PAGE_SIZE, N_HEADS, HEAD_DIM), index_map)

    def psum_map(i, j, pt):
        return (jnp.minimum(i * nj + j, n_groups - 1), 0, 0)

    def up_tile(i, j, pt):
        u = jnp.minimum(i * nj + j, n_up_tiles - 1)
        return (u // up_cols, u % up_cols)

    def down_tile(i, j, pt):
        d = jnp.clip(i * nj + j - n_up_tiles, 0, n_down_tiles - 1)
        return (d // down_cols, d % down_cols)

    tile = (_CAST_TILE, _CAST_TILE)
    grid_spec = pltpu.PrefetchScalarGridSpec(
        num_scalar_prefetch=1,
        grid=(t // tm, nj),
        in_specs=in_specs + [page_spec(u) for u in range(n_side)] + [
            pl.BlockSpec((None,) + tile, lambda i, j, pt: (next_layer,) + up_tile(i, j, pt)),
            pl.BlockSpec((None,) + tile, lambda i, j, pt: (next_layer,) + down_tile(i, j, pt)),
        ],
        out_specs=[out_specs, pl.BlockSpec((n_side, N_HEADS, HEAD_DIM), psum_map),
                   pl.BlockSpec(tile, up_tile), pl.BlockSpec(tile, down_tile)],
        scratch_shapes=scratch,
    )
    y, psum, up_bf, down_bf = pl.pallas_call(
        functools.partial(_ffn_kernel, n_side),
        grid_spec=grid_spec,
        out_shape=[out_shape, jax.ShapeDtypeStruct((n_groups * n_side, N_HEADS, HEAD_DIM), F32),
                   jax.ShapeDtypeStruct(next_up.shape[1:], BF16), jax.ShapeDtypeStruct(next_down.shape[1:], BF16)],
        compiler_params=_cparams(("arbitrary", "arbitrary")),
        name="ffn_side",
    )(pt_flat, *args, *([cache_k] * n_side), next_up, next_down)
    return y, psum[:count], up_bf, down_bf


_MXU_COLS = 256


def _dot_split(x, m):
    hi = x.astype(BF16)
    lo = (x - hi.astype(F32)).astype(BF16)
    return _dot(hi, m) + _dot(lo, m)


def _project_heads(xn_ref, w_ref, out_ref, g_ref, cos_ref, sin_ref):
    if g_ref is not None:
        r = lax.broadcasted_iota(jnp.int32, (_MXU_COLS, _MXU_COLS), 0)
        c = lax.broadcasted_iota(jnp.int32, (_MXU_COLS, _MXU_COLS), 1)
        same_head = (r // HEAD_DIM) == (c // HEAD_DIM)
        head_sum = same_head.astype(BF16)
        half_swap = (same_head & ((r % HEAD_DIM) == ((c + HEAD_DIM // 2) % HEAD_DIM))).astype(BF16)
        heads = _MXU_COLS // HEAD_DIM
        g = jnp.concatenate([g_ref[...]] * heads, axis=1)
    y_all = _dot(xn_ref[...], w_ref[...])
    if g_ref is None:
        out_ref[...] = y_all
        return
    for p in range(TN_PROJ // _MXU_COLS):
        y = y_all[:, p * _MXU_COLS:(p + 1) * _MXU_COLS]
        ms = _dot_split(y * y, head_sum) * (1.0 / HEAD_DIM)
        yn = y * lax.rsqrt(ms + EPS) * g
        rot = _dot_split(yn, half_swap)
        for hh in range(heads):
            lo = p * _MXU_COLS + hh * HEAD_DIM
            sl = slice(hh * HEAD_DIM, (hh + 1) * HEAD_DIM)
            out_ref[:, lo:lo + HEAD_DIM] = yn[:, sl] * cos_ref[...] + rot[:, sl] * sin_ref[...]


def _qkv_kernel(x_ref, g_ref, w_ref, gq_ref, gk_ref, cos_ref, sin_ref, q_ref, k_ref, v_ref, xn_ref):
    j = pl.program_id(1)
    n_q = D_MODEL // TN_PROJ

    @pl.when(j == 0)
    def _():
        xn_ref[...] = _rms_norm_rows(x_ref[...], g_ref[...]).astype(BF16)

    @pl.when(j < n_q)
    def _():
        _project_heads(xn_ref, w_ref, q_ref, gq_ref, cos_ref, sin_ref)

    @pl.when((j >= n_q) & (j < 2 * n_q))
    def _():
        _project_heads(xn_ref, w_ref, k_ref, gk_ref, cos_ref, sin_ref)

    @pl.when(j >= 2 * n_q)
    def _():
        _project_heads(xn_ref, w_ref, v_ref, None, None, None)


def _qkv(x, g_mix, w_qkv, g_q, g_k, cos, sin, layer, j_attn, tm):
    t = x.shape[0]
    n_q = D_MODEL // TN_PROJ
    n_pos_blocks = cos.shape[0] // tm
    out_sd = jax.ShapeDtypeStruct((t, D_MODEL), F32)
    return pl.pallas_call(
        _qkv_kernel,
        grid=(t // tm, 3 * n_q),
        in_specs=[
            pl.BlockSpec((tm, D_MODEL), lambda i, j: (i, 0)),
            pl.BlockSpec((None, 1, D_MODEL), lambda i, j: (layer, 0, 0)),
            pl.BlockSpec((None, D_MODEL, TN_PROJ), lambda i, j: (j_attn, 0, j)),
            pl.BlockSpec((None, 1, HEAD_DIM), lambda i, j: (j_attn, 0, 0)),
            pl.BlockSpec((None, 1, HEAD_DIM), lambda i, j: (j_attn, 0, 0)),
            pl.BlockSpec((tm, HEAD_DIM), lambda i, j: (i % n_pos_blocks, 0)),
            pl.BlockSpec((tm, HEAD_DIM), lambda i, j: (i % n_pos_blocks, 0)),
        ],
        out_specs=[
            pl.BlockSpec((tm, TN_PROJ), lambda i, j: (i, jnp.minimum(j, n_q - 1))),
            pl.BlockSpec((tm, TN_PROJ), lambda i, j: (i, jnp.clip(j - n_q, 0, n_q - 1))),
            pl.BlockSpec((tm, TN_PROJ), lambda i, j: (i, jnp.clip(j - 2 * n_q, 0, n_q - 1))),
        ],
        out_shape=[out_sd, out_sd, out_sd],
        scratch_shapes=[pltpu.VMEM((tm, D_MODEL), BF16)],
        compiler_params=_cparams(("parallel", "arbitrary")),
        name="qkv",
    )(x, g_mix, w_qkv, g_q, g_k, cos, sin)


_MASK_BIAS = -(2.0 ** 100)


def _moba_prompt_kernel(q_ref, k_ref, v_ref, o_ref, qa_ref, ka_ref, vb_ref):
    seq = k_ref.shape[0]
    nb = seq // MOBA_BLOCK
    q = q_ref[...]
    k = k_ref[...]

    kmean = jnp.mean(k.reshape(nb, MOBA_BLOCK, HEAD_DIM), axis=1)
    gate_t = _dot_nt(kmean, q, precision=HIGHEST)
    blk = lax.broadcasted_iota(jnp.int32, gate_t.shape, 0)
    qblk = lax.broadcasted_iota(jnp.int32, gate_t.shape, 1) // MOBA_BLOCK
    cnt = jnp.zeros(gate_t.shape, jnp.int32)
    for kk in range(nb):
        gk = gate_t[kk:kk + 1, :]
        beats = (gk > gate_t) | ((gk == gate_t) & (kk < blk))
        cnt = cnt + jnp.where(beats & (kk < qblk), 1, 0)
    allowed = ((cnt < MOBA_TOP_K) & (blk < qblk)) | (blk == qblk)
    bias_t = jnp.where(allowed, 0.0, _MASK_BIAS)
    eye = (lax.broadcasted_iota(jnp.int32, (nb, HEAD_DIM), 0)
           == lax.broadcasted_iota(jnp.int32, (nb, HEAD_DIM), 1)).astype(F32)
    bias_q = _dot_tn(bias_t, eye)

    key_blk = lax.broadcasted_iota(jnp.int32, (seq, HEAD_DIM), 0) // MOBA_BLOCK
    key_onehot = key_blk == lax.broadcasted_iota(jnp.int32, (seq, HEAD_DIM), 1)
    qa_ref[:, 0:HEAD_DIM] = q.astype(BF16)
    qa_ref[:, HEAD_DIM:2 * HEAD_DIM] = bias_q.astype(BF16)
    ka_ref[:, 0:HEAD_DIM] = k.astype(BF16)
    ka_ref[:, HEAD_DIM:2 * HEAD_DIM] = key_onehot.astype(BF16)
    vb_ref[...] = v_ref[...].astype(BF16)

    row = lax.broadcasted_iota(jnp.int32, (MOBA_BLOCK, MOBA_BLOCK), 0)
    col = lax.broadcasted_iota(jnp.int32, (MOBA_BLOCK, MOBA_BLOCK), 1)
    causal = col <= row
    exp2_scale = (HEAD_DIM ** -0.5) * 1.4426950408889634
    for i in range(nb):
        lo, hi = i * MOBA_BLOCK, (i + 1) * MOBA_BLOCK
        s = _dot_nt(qa_ref[lo:hi, :], ka_ref[0:hi, :])
        own = jnp.where(causal, s[:, lo:hi], _MASK_BIAS)
        s = own if i == 0 else jnp.concatenate([s[:, 0:lo], own], axis=1)
        m = jnp.max(s, axis=-1, keepdims=True)
        p = jnp.exp2((s - m) * exp2_scale)
        l = jnp.sum(p, axis=-1, keepdims=True)
        o_ref[lo:hi, :] = (_dot(p.astype(BF16), vb_ref[0:hi, :]) / l).astype(o_ref.dtype)


def _moba_prompt(q, k, v, batch, seq):
    blk = pl.BlockSpec((seq, HEAD_DIM), lambda b, h: (b, h))
    return pl.pallas_call(
        _moba_prompt_kernel,
        grid=(batch, N_HEADS),
        in_specs=[blk, blk, blk],
        out_specs=blk,
        out_shape=jax.ShapeDtypeStruct((batch * seq, N_HEADS * HEAD_DIM), BF16),
        scratch_shapes=[
            pltpu.VMEM((seq, 2 * HEAD_DIM), BF16),
            pltpu.VMEM((seq, 2 * HEAD_DIM), BF16),
            pltpu.VMEM((seq, HEAD_DIM), BF16),
        ],
        compiler_params=_cparams(("parallel", "parallel")),
        name="moba_prompt",
    )(q, k, v)


def _proj_res_kernel(a_ref, w_ref, x_ref, o_ref):
    o_ref[...] = x_ref[...] + _dot(a_ref[...].astype(BF16), w_ref[...])


def _proj_res(a, w, x, j_w, tm):
    t = x.shape[0]
    return pl.pallas_call(
        _proj_res_kernel,
        grid=(t // tm,),
        in_specs=[
            pl.BlockSpec((tm, D_MODEL), lambda i: (i, 0)),
            pl.BlockSpec((None, D_MODEL, D_MODEL), lambda i: (j_w, 0, 0)),
            pl.BlockSpec((tm, D_MODEL), lambda i: (i, 0)),
        ],
        out_specs=pl.BlockSpec((tm, D_MODEL), lambda i: (i, 0)),
        out_shape=jax.ShapeDtypeStruct((t, D_MODEL), F32),
        compiler_params=_cparams(("parallel",)),
        name="proj_res",
    )(a, w, x)


def _mlstm_in_kernel(x_ref, g_ref, w_ref, wg_ref, y_ref, gates_ref, xn_ref):
    j = pl.program_id(1)

    @pl.when(j == 0)
    def _():
        xn = _rms_norm_rows(x_ref[...], g_ref[...]).astype(BF16)
        xn_ref[...] = xn
        gates_ref[...] = _dot(xn, wg_ref[...])

    y_ref[...] = _dot(xn_ref[...], w_ref[...])


def _mlstm_in(x, g_mix, w_main, w_gates, layer, j_m, tm):
    t = x.shape[0]
    return pl.pallas_call(
        _mlstm_in_kernel,
        grid=(t // tm, M_MAIN // TN_PROJ),
        in_specs=[
            pl.BlockSpec((tm, D_MODEL), lambda i, j: (i, 0)),
            pl.BlockSpec((None, 1, D_MODEL), lambda i, j: (layer, 0, 0)),
            pl.BlockSpec((None, D_MODEL, TN_PROJ), lambda i, j: (j_m, 0, j)),
            pl.BlockSpec((None, D_MODEL, 128), lambda i, j: (j_m, 0, 0)),
        ],
        out_specs=[
            pl.BlockSpec((tm, TN_PROJ), lambda i, j: (i, j)),
            pl.BlockSpec((tm, 128), lambda i, j: (i, 0)),
        ],
        out_shape=[jax.ShapeDtypeStruct((t, M_MAIN), F32), jax.ShapeDtypeStruct((t, 128), F32)],
        scratch_shapes=[pltpu.VMEM((tm, D_MODEL), BF16)],
        compiler_params=_cparams(("parallel", "arbitrary")),
        name="mlstm_in",
    )(x, g_mix, w_main, w_gates)


_M_AUG = M_V_DIM + 128


def _cummax_lanes(x):
    n = x.shape[-1]
    lane = lax.broadcasted_iota(jnp.int32, x.shape, x.ndim - 1)
    if n % 128:
        out = x
        for s in range(n - 1):
            out = jnp.where(lane > s, jnp.maximum(out, x[:, s:s + 1]), out)
        return out
    shift = 1
    while shift < n:
        x = jnp.where(lane >= shift, jnp.maximum(x, pltpu.roll(x, shift, x.ndim - 1)), x)
        shift *= 2
    return x


def _mlstm_chunk_kernel(l_valid, zero_init, *refs):
    if zero_init:
        main_ref, gates_ref, bias_ref, gh_ref = refs[:4]
        rest = refs[4:]
    else:
        main_ref, gates_ref, bias_ref, gh_ref, c0_ref, n0_ref, m0_ref = refs[:7]
        rest = refs[7:]
    hn_ref, cout_ref, nout_ref, mout_ref, c_s, m_s = rest
    c = pl.program_id(1)
    L = main_ref.shape[0]
    scale_k = M_QK_DIM ** -0.5

    @pl.when(c == 0)
    def _():
        c_s[...] = jnp.zeros_like(c_s)
        if zero_init:
            m_s[...] = jnp.zeros_like(m_s)
        else:
            c_s[:, :, 0:M_V_DIM] = c0_ref[...]
            c_s[:, :, M_V_DIM:_M_AUG] = jnp.broadcast_to(n0_ref[...], (M_HEADS, M_QK_DIM, _M_AUG - M_V_DIM))
            for h in range(M_HEADS):
                m_s[h:h + 1, :] = jnp.broadcast_to(m0_ref[h], (1, 128))

    pre = gates_ref[...] + bias_ref[...]
    lane = lax.broadcasted_iota(jnp.int32, pre.shape, 1)
    act = jnp.where(lane < M_HEADS, IGATE_CAP * jnp.tanh(pre / IGATE_CAP), jax.nn.log_sigmoid(pre))
    r = lax.broadcasted_iota(jnp.int32, (L, L), 0)
    cc = lax.broadcasted_iota(jnp.int32, (L, L), 1)
    causal = cc <= r
    bcols = jnp.dot(causal.astype(F32), act, precision=HIGHEST, preferred_element_type=F32)
    act_t = act.T
    b_t = bcols.T
    rows = lax.broadcasted_iota(jnp.int32, (L, 1), 0)

    li_rows = act_t[0:M_HEADS, :]
    b_rows = b_t[M_HEADS:2 * M_HEADS, :]
    m_prev = m_s[:, 0:1]
    m_rows = b_rows + jnp.maximum(m_prev, _cummax_lanes(li_rows - b_rows))
    a_rows = jnp.exp(b_rows + m_prev - m_rows)
    stats = jnp.concatenate([m_rows, a_rows, jnp.exp(-m_rows),
                             jnp.zeros((128 - 3 * M_HEADS, L), F32)], axis=0).T
    ones_cols = jnp.ones((L, _M_AUG - M_V_DIM), BF16)

    for h in range(M_HEADS):
        q = main_ref[:, h * M_QK_DIM:(h + 1) * M_QK_DIM]
        k = main_ref[:, D_MQ + h * M_QK_DIM:D_MQ + (h + 1) * M_QK_DIM]
        v = main_ref[:, 2 * D_MQ + h * M_V_DIM:2 * D_MQ + (h + 1) * M_V_DIM]
        o = main_ref[:, 2 * D_MQ + D_MV + h * M_V_DIM:2 * D_MQ + D_MV + (h + 1) * M_V_DIM]
        b_col = bcols[:, M_HEADS + h:M_HEADS + h + 1]
        b_row = b_rows[h:h + 1, :]
        li_col = act[:, h:h + 1]
        li_row = li_rows[h:h + 1, :]
        m = stats[:, h:h + 1]
        a = stats[:, M_HEADS + h:M_HEADS + h + 1]
        exp_neg_m = stats[:, 2 * M_HEADS + h:2 * M_HEADS + h + 1]
        c0 = c_s[h]

        w = jnp.exp(jnp.where(causal, b_col - b_row + li_row, -jnp.inf) - m)
        qb = q.astype(BF16)
        kb = k.astype(BF16)
        v_aug = jnp.concatenate([v.astype(BF16), ones_cols], axis=1)
        s = _dot_nt(qb, kb) * scale_k * w
        nd = _dot(s.astype(BF16), v_aug) + a * _dot(qb, c0.astype(BF16))
        hh = nd[:, 0:M_V_DIM] / jnp.maximum(jnp.abs(nd[:, M_V_DIM:M_V_DIM + 1]), exp_neg_m)
        hn = hh * lax.rsqrt(jnp.mean(hh * hh, axis=-1, keepdims=True) + EPS)
        hn = hn * gh_ref[:, h * M_V_DIM:(h + 1) * M_V_DIM] * jax.nn.sigmoid(o)
        hn_ref[:, h * M_V_DIM:(h + 1) * M_V_DIM] = hn.astype(hn_ref.dtype)

        m_end = m[l_valid - 1:l_valid]
        a_end = a[l_valid - 1:l_valid]
        b_last = b_col[l_valid - 1:l_valid]
        wk = jnp.exp(b_last - b_col + li_col - m_end)
        if l_valid < L:
            wk = jnp.where(rows < l_valid, wk, 0.0)
        kw = k * (scale_k * wk)
        c_s[h] = a_end * c0 + _dot_tn(kw.astype(BF16), v_aug)
        m_s[h:h + 1, :] = jnp.broadcast_to(m_end, (1, 128))

    @pl.when(c == pl.num_programs(1) - 1)
    def _():
        cout_ref[...] = c_s[:, :, 0:M_V_DIM]
        nout_ref[...] = c_s[:, :, M_V_DIM:_M_AUG]
        mout_ref[...] = m_s[...]


def _mlstm_chunk(main, gates, bias, g_h, j_m, n_seq, n_chunks, chunk, l_valid, state=None):
    zero_init = state is None
    t = main.shape[0]
    in_specs = [
        pl.BlockSpec((chunk, M_MAIN), lambda n, c: (n * n_chunks + c, 0)),
        pl.BlockSpec((chunk, 128), lambda n, c: (n * n_chunks + c, 0)),
        pl.BlockSpec((None, 1, 128), lambda n, c: (j_m, 0, 0)),
        pl.BlockSpec((None, 1, D_MV), lambda n, c: (j_m, 0, 0)),
    ]
    args = [main, gates, bias, g_h]
    if not zero_init:
        in_specs += [
            pl.BlockSpec((None, None, M_HEADS, M_QK_DIM, M_V_DIM), lambda n, c: (j_m, n, 0, 0, 0)),
            pl.BlockSpec((None, None, M_HEADS, M_QK_DIM, 1), lambda n, c: (j_m, n, 0, 0, 0)),
            pl.BlockSpec((None, None, M_HEADS, 1, 1), lambda n, c: (j_m, n, 0, 0, 0)),
        ]
        args += list(state)
    hn_dtype = BF16 if chunk % 16 == 0 else F32
    return pl.pallas_call(
        functools.partial(_mlstm_chunk_kernel, l_valid, zero_init),
        grid=(n_seq, n_chunks),
        in_specs=in_specs,
        out_specs=[
            pl.BlockSpec((chunk, D_MV), lambda n, c: (n * n_chunks + c, 0)),
            pl.BlockSpec((None, M_HEADS, M_QK_DIM, M_V_DIM), lambda n, c: (n, 0, 0, 0)),
            pl.BlockSpec((None, M_HEADS, M_QK_DIM, _M_AUG - M_V_DIM), lambda n, c: (n, 0, 0, 0)),
            pl.BlockSpec((None, M_HEADS, 128), lambda n, c: (n, 0, 0)),
        ],
        out_shape=[
            jax.ShapeDtypeStruct((t, D_MV), hn_dtype),
            jax.ShapeDtypeStruct((n_seq, M_HEADS, M_QK_DIM, M_V_DIM), F32),
            jax.ShapeDtypeStruct((n_seq, M_HEADS, M_QK_DIM, _M_AUG - M_V_DIM), F32),
            jax.ShapeDtypeStruct((n_seq, M_HEADS, 128), F32),
        ],
        scratch_shapes=[
            pltpu.VMEM((M_HEADS, M_QK_DIM, _M_AUG), F32),
            pltpu.VMEM((M_HEADS, 128), F32),
        ],
        compiler_params=_cparams(("parallel", "arbitrary")),
        name="mlstm_chunk",
    )(*args)


def _sample_select_kernel(q_ref, psum_ref, sel_ref, kmean_ref):
    c_past = kmean_ref.shape[1]

    for c in range(c_past):
        tot = psum_ref[c * PAGES_PER_BLOCK]
        for pp in range(1, PAGES_PER_BLOCK):
            tot = tot + psum_ref[c * PAGES_PER_BLOCK + pp]
        tot = tot * (1.0 / MOBA_BLOCK)
        for h in range(N_HEADS):
            kmean_ref[h, c:c + 1, :] = tot[h:h + 1, :]

    out_lane = lax.broadcasted_iota(jnp.int32, sel_ref.shape, 1)
    out = jnp.zeros(sel_ref.shape, jnp.int32)
    for h in range(N_HEADS):
        g = _dot_nt(q_ref[:, h * HEAD_DIM:(h + 1) * HEAD_DIM], kmean_ref[h], precision=HIGHEST)
        lane = lax.broadcasted_iota(jnp.int32, g.shape, 1)
        for kk in range(MOBA_TOP_K):
            mx = jnp.max(g, axis=-1, keepdims=True)
            idx = jnp.min(jnp.where(g == mx, lane, c_past), axis=-1, keepdims=True)
            out = jnp.where(out_lane == h * MOBA_TOP_K + kk, idx, out)
            g = jnp.where(lane == idx, -jnp.inf, g)
    sel_ref[...] = out


def _sample_select(psum, q, n_db):
    n_pages = psum.shape[1]
    c_past = n_pages // PAGES_PER_BLOCK
    return pl.pallas_call(
        _sample_select_kernel,
        grid=(n_db,),
        in_specs=[
            pl.BlockSpec((SAMPLE_T_PAD, N_HEADS * HEAD_DIM), lambda db: (db, 0)),
            pl.BlockSpec((None, n_pages, N_HEADS, HEAD_DIM), lambda db: (db, 0, 0, 0)),
        ],
        out_specs=pl.BlockSpec((None, SAMPLE_T_PAD, 128), lambda db: (db, 0, 0)),
        out_shape=jax.ShapeDtypeStruct((n_db, SAMPLE_T_PAD, 128), jnp.int32),
        scratch_shapes=[pltpu.VMEM((N_HEADS, c_past, HEAD_DIM), F32)],
        compiler_params=_cparams(("parallel",)),
        name="sample_select",
    )(q, psum)


_N_SEL_PAGES = MOBA_TOP_K * PAGES_PER_BLOCK


def _sample_attn_kernel(t_valid, j_attn, pt_ref, sel_ref, q_ref, kn_ref, vn_ref, ck_ref, cv_ref, o_ref,
                        kbuf, vbuf, sem):
    n_heads = pl.num_programs(1)
    step = pl.program_id(0) * n_heads + pl.program_id(1)
    n_steps = pl.num_programs(0) * n_heads
    slot = step % 2
    scale = HEAD_DIM ** -0.5

    def copies(step_, slot_):
        db_ = step_ // n_heads
        h_ = step_ % n_heads
        out = []
        for t in range(t_valid):
            for kk in range(MOBA_TOP_K):
                blk = sel_ref[db_ * (SAMPLE_T_PAD * 128) + t * 128 + h_ * MOBA_TOP_K + kk]
                for pp in range(PAGES_PER_BLOCK):
                    page = pt_ref[db_, blk * PAGES_PER_BLOCK + pp]
                    i = (t * MOBA_TOP_K + kk) * PAGES_PER_BLOCK + pp
                    out.append(pltpu.make_async_copy(ck_ref.at[j_attn, page, :, h_, :], kbuf.at[slot_, i],
                                                     sem.at[0, slot_]))
                    out.append(pltpu.make_async_copy(cv_ref.at[j_attn, page, :, h_, :], vbuf.at[slot_, i],
                                                     sem.at[1, slot_]))
        return out

    def start_all(step_, slot_):
        for i, c in enumerate(copies(step_, slot_)):
            c.start(priority=i % 2)

    @pl.when(step == 0)
    def _():
        start_all(step, slot)

    @pl.when(step + 1 < n_steps)
    def _():
        start_all(step + 1, 1 - slot)

    for c in copies(step, slot):
        c.wait()

    kn = kn_ref[...].astype(BF16)
    vn = vn_ref[...].astype(BF16)
    lane = lax.broadcasted_iota(jnp.int32, (SAMPLE_T_PAD, SAMPLE_T_PAD), 1)
    rows = []
    for t in range(t_valid):
        q = jnp.broadcast_to(q_ref[t:t + 1, :], (SAMPLE_T_PAD, HEAD_DIM)).astype(BF16)
        ks = kbuf[slot, t * _N_SEL_PAGES:(t + 1) * _N_SEL_PAGES].reshape(_N_SEL_PAGES * PAGE_SIZE, HEAD_DIM)
        vs = vbuf[slot, t * _N_SEL_PAGES:(t + 1) * _N_SEL_PAGES].reshape(_N_SEL_PAGES * PAGE_SIZE, HEAD_DIM)
        s_sel = _dot_nt(q, ks.astype(BF16)) * scale
        s_own = jnp.where(lane <= t, _dot_nt(q, kn) * scale, -jnp.inf)
        m = jnp.maximum(jnp.max(s_sel, axis=-1, keepdims=True), jnp.max(s_own, axis=-1, keepdims=True))
        p_sel = jnp.exp(s_sel - m)
        p_own = jnp.exp(s_own - m)
        l = jnp.sum(p_sel, axis=-1, keepdims=True) + jnp.sum(p_own, axis=-1, keepdims=True)
        out = (_dot(p_sel.astype(BF16), vs.astype(BF16)) + _dot(p_own.astype(BF16), vn)) / l
        rows.append(out[0:1, :])
    rows.append(jnp.zeros((SAMPLE_T_PAD - t_valid, HEAD_DIM), F32))
    o_ref[...] = jnp.concatenate(rows, axis=0)


def _sample_attn(page_table, sel_flat, q, k_new, v_new, cache_k, cache_v, j_attn, n_db, t_valid):
    row_spec = pl.BlockSpec((SAMPLE_T_PAD, HEAD_DIM), lambda db, h, pt, sel: (db, h))
    hbm_spec = pl.BlockSpec(memory_space=pl.ANY)
    n_slices = t_valid * _N_SEL_PAGES
    grid_spec = pltpu.PrefetchScalarGridSpec(
        num_scalar_prefetch=2,
        grid=(n_db, N_HEADS),
        in_specs=[row_spec, row_spec, row_spec, hbm_spec, hbm_spec],
        out_specs=row_spec,
        scratch_shapes=[
            pltpu.VMEM((2, n_slices, PAGE_SIZE, HEAD_DIM), F32),
            pltpu.VMEM((2, n_slices, PAGE_SIZE, HEAD_DIM), F32),
            pltpu.SemaphoreType.DMA((2, 2)),
        ],
    )
    return pl.pallas_call(
        functools.partial(_sample_attn_kernel, t_valid, j_attn),
        grid_spec=grid_spec,
        out_shape=jax.ShapeDtypeStruct((n_db * SAMPLE_T_PAD, N_HEADS * HEAD_DIM), F32),
        compiler_params=_cparams(("arbitrary", "arbitrary")),
        name="sample_attn",
    )(page_table, sel_flat, q, k_new, v_new, cache_k, cache_v)


def _rope_tables(pos):
    half = HEAD_DIM // 2
    inv = ROPE_THETA ** (-jnp.arange(half, dtype=F32) / half)
    ang = pos.astype(F32)[:, None] * inv[None, :]
    cos = jnp.cos(ang)
    sin = jnp.sin(ang)
    return jnp.concatenate([cos, cos], axis=-1), jnp.concatenate([-sin, sin], axis=-1)


def kernel(x_prompt, x_sample, cache_k, cache_v, state_C, state_n, state_m, page_table, g_ffn1, w_ffn1_up, w_ffn1_down, g_mix, attn_w_qkv, attn_g_q, attn_g_k, attn_w_o, mlstm_w_in, mlstm_b_if, mlstm_g_h, mlstm_w_out, g_ffn2, w_ffn2_up, w_ffn2_down):
    b_, s_, d = x_prompt.shape
    db, t_s, _ = x_sample.shape
    assert d == D_MODEL and s_ % MOBA_BLOCK == 0 and s_ % TM_PROJ == 0 and s_ % M_CHUNK == 0 and t_s <= SAMPLE_T_PAD
    assert PAST_LEN % MOBA_BLOCK == 0 and page_table.shape[1] * PAGE_SIZE == PAST_LEN
    n_mlstm = mlstm_w_in.shape[0]

    ffn_w = [(w_ffn1_up[0].astype(BF16), w_ffn1_down[0].astype(BF16))]
    wqkv, wo = attn_w_qkv.astype(BF16), attn_w_o.astype(BF16)
    w_in_main = mlstm_w_in.astype(BF16)
    w_in_gates = jnp.pad(mlstm_w_in[:, :, M_MAIN:], ((0, 0), (0, 0), (0, 128 - 2 * M_HEADS))).astype(BF16)
    w_out = mlstm_w_out.astype(BF16)
    g1 = g_ffn1.reshape(DEPTH, 1, d)
    g2 = g_ffn2.reshape(DEPTH, 1, d)
    gm = g_mix.reshape(DEPTH, 1, d)
    gq = attn_g_q.reshape(-1, 1, HEAD_DIM)
    gk = attn_g_k.reshape(-1, 1, HEAD_DIM)
    b_if = jnp.pad(mlstm_b_if, ((0, 0), (0, 128 - 2 * M_HEADS))).reshape(n_mlstm, 1, 128)
    g_h = mlstm_g_h.reshape(n_mlstm, 1, D_MV)

    cos_p, sin_p = _rope_tables(jnp.arange(s_))
    cos_s, sin_s = _rope_tables(PAST_LEN + (jnp.arange(db * SAMPLE_T_PAD) % SAMPLE_T_PAD))

    st_n = state_n.reshape(n_mlstm, db, M_HEADS, M_QK_DIM, 1)
    st_m = state_m.reshape(n_mlstm, db, M_HEADS, 1, 1)

    xp = x_prompt.reshape(b_ * s_, d)
    xs = jnp.pad(x_sample, ((0, 0), (0, SAMPLE_T_PAD - t_s), (0, 0))).reshape(db * SAMPLE_T_PAD, d)
    tm_s = db * SAMPLE_T_PAD

    n_attn = cache_k.shape[0]
    pt_flat = page_table.reshape(-1)
    pages_per_layer = pt_flat.shape[0]
    pages_per_call = n_attn * pages_per_layer // (2 * DEPTH)
    calls_per_layer = pages_per_layer // pages_per_call
    assert pages_per_call * 2 * DEPTH == n_attn * pages_per_layer and calls_per_layer * pages_per_call == pages_per_layer
    psums = [[] for _ in range(n_attn)]

    def ffn_prompt(x, i, second):
        call = 2 * i + second
        j_attn = call // calls_per_layer
        if second:
            nxt = (w_ffn1_up, w_ffn1_down, (i + 1) % DEPTH)
        else:
            nxt = (w_ffn2_up, w_ffn2_down, i)
        side = (pt_flat, cache_k, j_attn, (call % calls_per_layer) * pages_per_call, pages_per_call) + nxt
        y, ps, up_bf, down_bf = _ffn(x, g2 if second else g1, *ffn_w[call], i, TM_PROMPT, side=side)
        psums[j_attn].append(ps)
        ffn_w.append((up_bf, down_bf))
        return y

    kp_l, vp_l, ks_l, vs_l = [], [], [], []
    cp_l, np_l, mp_l, cs_l, ns_l, ms_l = [], [], [], [], [], []
    for i in range(DEPTH):
        xp = ffn_prompt(xp, i, 0)
        j = i // 2
        if i % 2 == 0:
            qp, kp, vp = _qkv(xp, gm, wqkv, gq, gk, cos_p, sin_p, i, j, TM_PROJ)
            op = _moba_prompt(qp, kp, vp, b_, s_)
            xp = _proj_res(op, wo, xp, j, TM_PROMPT)
            kp_l.append(kp.reshape(b_, s_ // PAGE_SIZE, PAGE_SIZE, N_HEADS, HEAD_DIM))
            vp_l.append(vp.reshape(b_, s_ // PAGE_SIZE, PAGE_SIZE, N_HEADS, HEAD_DIM))
        else:
            main, gates = _mlstm_in(xp, gm, w_in_main, w_in_gates, i, j, TM_PROJ)
            hn, c_p, n_p, m_p = _mlstm_chunk(main, gates, b_if, g_h, j, b_, s_ // M_CHUNK, M_CHUNK, M_CHUNK)
            xp = _proj_res(hn, w_out, xp, j, TM_PROMPT)
            cp_l.append(c_p)
            np_l.append(n_p[:, :, :, 0])
            mp_l.append(m_p[:, :, 0])
        xp = ffn_prompt(xp, i, 1)

    psum_l = [jnp.concatenate(p, axis=0).reshape(db, -1, N_HEADS, HEAD_DIM) for p in psums]
    for i in range(DEPTH):
        xs = _ffn(xs, g1, *ffn_w[2 * i], i, tm_s)
        j = i // 2
        if i % 2 == 0:
            qs, ks, vs = _qkv(xs, gm, wqkv, gq, gk, cos_s, sin_s, i, j, tm_s)
            sel = _sample_select(psum_l[j], qs, db)
            os_ = _sample_attn(page_table, sel.reshape(-1), qs, ks, vs, cache_k, cache_v, j, db, t_s)
            xs = _proj_res(os_, wo, xs, j, tm_s)
            ks_l.append(ks.reshape(db, SAMPLE_T_PAD, N_HEADS, HEAD_DIM)[:, :t_s])
            vs_l.append(vs.reshape(db, SAMPLE_T_PAD, N_HEADS, HEAD_DIM)[:, :t_s])
        else:
            main, gates = _mlstm_in(xs, gm, w_in_main, w_in_gates, i, j, tm_s)
            hn, c_s, n_s, m_s = _mlstm_chunk(main, gates, b_if, g_h, j, db, 1, SAMPLE_T_PAD, t_s,
                                             state=(state_C, st_n, st_m))
            xs = _proj_res(hn, w_out, xs, j, tm_s)
            cs_l.append(c_s)
            ns_l.append(n_s[:, :, :, 0])
            ms_l.append(m_s[:, :, 0])
        xs = _ffn(xs, g2, *ffn_w[2 * i + 1], i, tm_s)

    y_prompt = xp.reshape(b_, s_, d)
    y_sample = xs.reshape(db, SAMPLE_T_PAD, d)[:, :t_s]
    return (y_prompt, y_sample, jnp.stack(kp_l), jnp.stack(vp_l), jnp.stack(ks_l), jnp.stack(vs_l),
            jnp.stack(cp_l), jnp.stack(np_l), jnp.stack(mp_l), jnp.stack(cs_l), jnp.stack(ns_l), jnp.stack(ms_l))
```

```python
import functools

import jax
import jax.numpy as jnp
from jax import lax
from jax.experimental import pallas as pl
from jax.experimental.pallas import tpu as pltpu

F32 = jnp.float32
BF16 = jnp.bfloat16
HIGHEST = lax.Precision.HIGHEST

D_MODEL = 2048
DEPTH = 4
PAST_LEN = 16384
PAGE_SIZE = 128
N_HEADS = 16
HEAD_DIM = 128
MOBA_BLOCK = 256
MOBA_TOP_K = 3
ROPE_THETA = 10000.0
M_HEADS = 8
M_QK_DIM = 128
M_V_DIM = 256
M_CHUNK = 256
IGATE_CAP = 15.0
D_FF = 5632
EPS = 1e-6

D_MQ = M_HEADS * M_QK_DIM
D_MV = M_HEADS * M_V_DIM
M_MAIN = 2 * D_MQ + 2 * D_MV
SAMPLE_T_PAD = 8
PAGES_PER_BLOCK = MOBA_BLOCK // PAGE_SIZE

TM_PROMPT = 512
TM_PROJ = 1024
TF_FFN = 512
TN_PROJ = 512
VMEM_LIMIT = 56 * 1024 * 1024


def _cparams(sem):
    return pltpu.CompilerParams(dimension_semantics=sem, vmem_limit_bytes=VMEM_LIMIT)


def _dot(a, b):
    return jnp.dot(a, b, preferred_element_type=F32)


def _dot_nt(a, b, precision=None):
    return lax.dot_general(a, b, (((1,), (1,)), ((), ())), precision=precision,
                           preferred_element_type=F32)


def _dot_tn(a, b):
    return lax.dot_general(a, b, (((0,), (0,)), ((), ())), preferred_element_type=F32)


def _rms_norm_rows(x, g):
    ms = jnp.mean(x * x, axis=-1, keepdims=True)
    return x * lax.rsqrt(ms + EPS) * g


_FFN_SIDE_PAGES = 2
_CAST_TILE = 512


def _ffn_kernel(n_side, n_cast, *refs):
    if n_side:
        refs = refs[1:]
    x_ref, g_ref, wg_ref, wu_ref, wd_ref = refs[:5]
    refs = refs[5:]
    if n_side:
        page_refs, cast_in = refs[:n_side], refs[n_side:n_side + n_cast]
        o_ref, psum_ref = refs[n_side + n_cast:n_side + n_cast + 2]
        cast_out = refs[n_side + n_cast + 2:n_side + 2 * n_cast + 2]
        xn_ref, acc_ref = refs[n_side + 2 * n_cast + 2:]
    else:
        o_ref, xn_ref, acc_ref = refs
    j = pl.program_id(1)

    @pl.when(j == 0)
    def _():
        xn_ref[...] = _rms_norm_rows(x_ref[...], g_ref[...]).astype(BF16)
        acc_ref[...] = jnp.zeros_like(acc_ref)

    xn = xn_ref[...]
    gate = _dot(xn, wg_ref[...])
    up = _dot(xn, wu_ref[...])
    h = (gate * jax.nn.sigmoid(gate) * up).astype(BF16)
    acc_ref[...] += _dot(h, wd_ref[...])

    if n_side:
        for u in range(n_side):
            psum_ref[u] = jnp.sum(page_refs[u][...], axis=0)
        for src_ref, dst_ref in zip(cast_in, cast_out):
            dst_ref[...] = src_ref[...].astype(BF16)

    @pl.when(j == pl.num_programs(1) - 1)
    def _():
        o_ref[...] = x_ref[...] + 0.5 * acc_ref[...]


def _ffn(x, g, w_up, w_down, layer, tm, side=None):
    t = x.shape[0]
    nj = D_FF // TF_FFN
    n_steps = (t // tm) * nj
    in_specs = [
        pl.BlockSpec((tm, D_MODEL), lambda i, j, *_: (i, 0)),
        pl.BlockSpec((None, 1, D_MODEL), lambda i, j, *_: (layer, 0, 0)),
        pl.BlockSpec((D_MODEL, TF_FFN), lambda i, j, *_: (0, j)),
        pl.BlockSpec((D_MODEL, TF_FFN), lambda i, j, *_: (0, j + nj)),
        pl.BlockSpec((TF_FFN, D_MODEL), lambda i, j, *_: (j, 0)),
    ]
    out_specs = pl.BlockSpec((tm, D_MODEL), lambda i, j, *_: (i, 0))
    out_shape = jax.ShapeDtypeStruct((t, D_MODEL), F32)
    scratch = [pltpu.VMEM((tm, D_MODEL), BF16), pltpu.VMEM((tm, D_MODEL), F32)]
    args = [x, g, w_up, w_up, w_down]
    if side is None:
        return pl.pallas_call(
            functools.partial(_ffn_kernel, 0, 0),
            grid=(t // tm, nj), in_specs=in_specs, out_specs=out_specs, out_shape=out_shape,
            scratch_shapes=scratch, compiler_params=_cparams(("parallel", "arbitrary")), name="ffn",
        )(*args)

    pt_flat, cache_k, j_attn, first, count, next_up, next_down, next_layer = side
    n_side = _FFN_SIDE_PAGES
    n_groups = -(-count // n_side)
    up_cols, down_cols = 2 * D_FF // _CAST_TILE, D_MODEL // _CAST_TILE
    n_up_tiles = (D_MODEL // _CAST_TILE) * up_cols
    n_down_tiles = (D_FF // _CAST_TILE) * down_cols
    assert n_steps >= n_groups and n_steps >= n_up_tiles + n_down_tiles

    def page_spec(u):
        def index_map(i, j, pt):
            group = jnp.minimum(i * nj + j, n_groups - 1)
            return (j_attn, pt[first + jnp.minimum(group * n_side + u, count - 1)], 0, 0, 0)
        return pl.BlockSpec((None, None, PAGE_SIZE, N_HEADS, HEAD_DIM), index_map)

    def psum_map(i, j, pt):
        return (jnp.minimum(i * nj + j, n_groups - 1), 0, 0)

    def up_tile(i, j, pt):
        u = jnp.minimum(i * nj + j, n_up_tiles - 1)
        return (u // up_cols, u % up_cols)

    def down_tile(i, j, pt):
        d = jnp.clip(i * nj + j - n_up_tiles, 0, n_down_tiles - 1)
        return (d // down_cols, d % down_cols)

    tile = (_CAST_TILE, _CAST_TILE)
    cast_args, cast_in_specs, cast_out_specs, cast_out_shapes = [], [], [], []
    if next_up is not None:
        cast_args = [next_up, next_down]
        cast_in_specs = [
            pl.BlockSpec((None,) + tile, lambda i, j, pt: (next_layer,) + up_tile(i, j, pt)),
            pl.BlockSpec((None,) + tile, lambda i, j, pt: (next_layer,) + down_tile(i, j, pt)),
        ]
        cast_out_specs = [pl.BlockSpec(tile, up_tile), pl.BlockSpec(tile, down_tile)]
        cast_out_shapes = [jax.ShapeDtypeStruct(next_up.shape[1:], BF16), jax.ShapeDtypeStruct(next_down.shape[1:], BF16)]
    grid_spec = pltpu.PrefetchScalarGridSpec(
        num_scalar_prefetch=1,
        grid=(t // tm, nj),
        in_specs=in_specs + [page_spec(u) for u in range(n_side)] + cast_in_specs,
        out_specs=[out_specs, pl.BlockSpec((n_side, N_HEADS, HEAD_DIM), psum_map)] + cast_out_specs,
        scratch_shapes=scratch,
    )
    y, psum, *cast_bf = pl.pallas_call(
        functools.partial(_ffn_kernel, n_side, len(cast_args)),
        grid_spec=grid_spec,
        out_shape=[out_shape, jax.ShapeDtypeStruct((n_groups * n_side, N_HEADS, HEAD_DIM), F32)] + cast_out_shapes,
        compiler_params=_cparams(("arbitrary", "arbitrary")),
        name="ffn_side",
    )(pt_flat, *args, *([cache_k] * n_side), *cast_args)
    up_bf, down_bf = cast_bf if cast_bf else (None, None)
    return y, psum[:count], up_bf, down_bf


_MXU_COLS = 256


def _dot_split(x, m):
    hi = x.astype(BF16)
    lo = (x - hi.astype(F32)).astype(BF16)
    return _dot(hi, m) + _dot(lo, m)


def _project_heads(xn_ref, w_ref, out_ref, g_ref, cos_ref, sin_ref):
    if g_ref is not None:
        r = lax.broadcasted_iota(jnp.int32, (_MXU_COLS, _MXU_COLS), 0)
        c = lax.broadcasted_iota(jnp.int32, (_MXU_COLS, _MXU_COLS), 1)
        same_head = (r // HEAD_DIM) == (c // HEAD_DIM)
        head_sum = same_head.astype(BF16)
        half_swap = (same_head & ((r % HEAD_DIM) == ((c + HEAD_DIM // 2) % HEAD_DIM))).astype(BF16)
        heads = _MXU_COLS // HEAD_DIM
        g = jnp.concatenate([g_ref[...]] * heads, axis=1)
    y_all = _dot(xn_ref[...], w_ref[...])
    if g_ref is None:
        out_ref[...] = y_all
        return
    for p in range(TN_PROJ // _MXU_COLS):
        y = y_all[:, p * _MXU_COLS:(p + 1) * _MXU_COLS]
        ms = _dot_split(y * y, head_sum) * (1.0 / HEAD_DIM)
        yn = y * lax.rsqrt(ms + EPS) * g
        rot = _dot_split(yn, half_swap)
        for hh in range(heads):
            lo = p * _MXU_COLS + hh * HEAD_DIM
            sl = slice(hh * HEAD_DIM, (hh + 1) * HEAD_DIM)
            out_ref[:, lo:lo + HEAD_DIM] = yn[:, sl] * cos_ref[...] + rot[:, sl] * sin_ref[...]


def _qkv_kernel(x_ref, g_ref, w_ref, gq_ref, gk_ref, cos_ref, sin_ref, q_ref, k_ref, v_ref, xn_ref):
    j = pl.program_id(1)
    n_q = D_MODEL // TN_PROJ

    @pl.when(j == 0)
    def _():
        xn_ref[...] = _rms_norm_rows(x_ref[...], g_ref[...]).astype(BF16)

    @pl.when(j < n_q)
    def _():
        _project_heads(xn_ref, w_ref, q_ref, gq_ref, cos_ref, sin_ref)

    @pl.when((j >= n_q) & (j < 2 * n_q))
    def _():
        _project_heads(xn_ref, w_ref, k_ref, gk_ref, cos_ref, sin_ref)

    @pl.when(j >= 2 * n_q)
    def _():
        _project_heads(xn_ref, w_ref, v_ref, None, None, None)


def _qkv(x, g_mix, w_qkv, g_q, g_k, cos, sin, layer, j_attn, tm):
    t = x.shape[0]
    n_q = D_MODEL // TN_PROJ
    n_pos_blocks = cos.shape[0] // tm
    out_sd = jax.ShapeDtypeStruct((t, D_MODEL), F32)
    return pl.pallas_call(
        _qkv_kernel,
        grid=(t // tm, 3 * n_q),
        in_specs=[
            pl.BlockSpec((tm, D_MODEL), lambda i, j: (i, 0)),
            pl.BlockSpec((None, 1, D_MODEL), lambda i, j: (layer, 0, 0)),
            pl.BlockSpec((None, D_MODEL, TN_PROJ), lambda i, j: (j_attn, 0, j)),
            pl.BlockSpec((None, 1, HEAD_DIM), lambda i, j: (j_attn, 0, 0)),
            pl.BlockSpec((None, 1, HEAD_DIM), lambda i, j: (j_attn, 0, 0)),
            pl.BlockSpec((tm, HEAD_DIM), lambda i, j: (i % n_pos_blocks, 0)),
            pl.BlockSpec((tm, HEAD_DIM), lambda i, j: (i % n_pos_blocks, 0)),
        ],
        out_specs=[
            pl.BlockSpec((tm, TN_PROJ), lambda i, j: (i, jnp.minimum(j, n_q - 1))),
            pl.BlockSpec((tm, TN_PROJ), lambda i, j: (i, jnp.clip(j - n_q, 0, n_q - 1))),
            pl.BlockSpec((tm, TN_PROJ), lambda i, j: (i, jnp.clip(j - 2 * n_q, 0, n_q - 1))),
        ],
        out_shape=[out_sd, out_sd, out_sd],
        scratch_shapes=[pltpu.VMEM((tm, D_MODEL), BF16)],
        compiler_params=_cparams(("parallel", "arbitrary")),
        name="qkv",
    )(x, g_mix, w_qkv, g_q, g_k, cos, sin)


_MASK_BIAS = -(2.0 ** 100)


def _moba_prompt_kernel(q_ref, k_ref, v_ref, o_ref, qa_ref, ka_ref, vb_ref):
    seq = k_ref.shape[0]
    nb = seq // MOBA_BLOCK
    q = q_ref[...]
    k = k_ref[...]

    kmean = jnp.mean(k.reshape(nb, MOBA_BLOCK, HEAD_DIM), axis=1)
    gate_t = _dot_nt(kmean, q, precision=HIGHEST)
    blk = lax.broadcasted_iota(jnp.int32, gate_t.shape, 0)
    qblk = lax.broadcasted_iota(jnp.int32, gate_t.shape, 1) // MOBA_BLOCK
    cnt = jnp.zeros(gate_t.shape, jnp.int32)
    for kk in range(nb):
        gk = gate_t[kk:kk + 1, :]
        beats = (gk > gate_t) | ((gk == gate_t) & (kk < blk))
        cnt = cnt + jnp.where(beats & (kk < qblk), 1, 0)
    allowed = ((cnt < MOBA_TOP_K) & (blk < qblk)) | (blk == qblk)
    bias_t = jnp.where(allowed, 0.0, _MASK_BIAS)
    eye = (lax.broadcasted_iota(jnp.int32, (nb, HEAD_DIM), 0)
           == lax.broadcasted_iota(jnp.int32, (nb, HEAD_DIM), 1)).astype(F32)
    bias_q = _dot_tn(bias_t, eye)

    key_blk = lax.broadcasted_iota(jnp.int32, (seq, HEAD_DIM), 0) // MOBA_BLOCK
    key_onehot = key_blk == lax.broadcasted_iota(jnp.int32, (seq, HEAD_DIM), 1)
    qa_ref[:, 0:HEAD_DIM] = q.astype(BF16)
    qa_ref[:, HEAD_DIM:2 * HEAD_DIM] = bias_q.astype(BF16)
    ka_ref[:, 0:HEAD_DIM] = k.astype(BF16)
    ka_ref[:, HEAD_DIM:2 * HEAD_DIM] = key_onehot.astype(BF16)
    vb_ref[...] = v_ref[...].astype(BF16)

    row = lax.broadcasted_iota(jnp.int32, (MOBA_BLOCK, MOBA_BLOCK), 0)
    col = lax.broadcasted_iota(jnp.int32, (MOBA_BLOCK, MOBA_BLOCK), 1)
    causal = col <= row
    exp2_scale = (HEAD_DIM ** -0.5) * 1.4426950408889634
    for i in range(nb):
        lo, hi = i * MOBA_BLOCK, (i + 1) * MOBA_BLOCK
        s = _dot_nt(qa_ref[lo:hi, :], ka_ref[0:hi, :])
        own = jnp.where(causal, s[:, lo:hi], _MASK_BIAS)
        s = own if i == 0 else jnp.concatenate([s[:, 0:lo], own], axis=1)
        m = jnp.max(s, axis=-1, keepdims=True)
        p = jnp.exp2((s - m) * exp2_scale)
        l = jnp.sum(p, axis=-1, keepdims=True)
        o_ref[lo:hi, :] = (_dot(p.astype(BF16), vb_ref[0:hi, :]) / l).astype(o_ref.dtype)


def _moba_prompt(q, k, v, batch, seq):
    blk = pl.BlockSpec((seq, HEAD_DIM), lambda b, h: (b, h))
    return pl.pallas_call(
        _moba_prompt_kernel,
        grid=(batch, N_HEADS),
        in_specs=[blk, blk, blk],
        out_specs=blk,
        out_shape=jax.ShapeDtypeStruct((batch * seq, N_HEADS * HEAD_DIM), BF16),
        scratch_shapes=[
            pltpu.VMEM((seq, 2 * HEAD_DIM), BF16),
            pltpu.VMEM((seq, 2 * HEAD_DIM), BF16),
            pltpu.VMEM((seq, HEAD_DIM), BF16),
        ],
        compiler_params=_cparams(("parallel", "parallel")),
        name="moba_prompt",
    )(q, k, v)


def _proj_res_kernel(a_ref, w_ref, x_ref, o_ref):
    o_ref[...] = x_ref[...] + _dot(a_ref[...].astype(BF16), w_ref[...])


def _proj_res(a, w, x, j_w, tm):
    t = x.shape[0]
    return pl.pallas_call(
        _proj_res_kernel,
        grid=(t // tm,),
        in_specs=[
            pl.BlockSpec((tm, D_MODEL), lambda i: (i, 0)),
            pl.BlockSpec((None, D_MODEL, D_MODEL), lambda i: (j_w, 0, 0)),
            pl.BlockSpec((tm, D_MODEL), lambda i: (i, 0)),
        ],
        out_specs=pl.BlockSpec((tm, D_MODEL), lambda i: (i, 0)),
        out_shape=jax.ShapeDtypeStruct((t, D_MODEL), F32),
        compiler_params=_cparams(("parallel",)),
        name="proj_res",
    )(a, w, x)


def _mlstm_in_kernel(x_ref, g_ref, w_ref, wg_ref, y_ref, gates_ref, xn_ref):
    j = pl.program_id(1)

    @pl.when(j == 0)
    def _():
        xn = _rms_norm_rows(x_ref[...], g_ref[...]).astype(BF16)
        xn_ref[...] = xn
        gates_ref[...] = _dot(xn, wg_ref[...])

    y_ref[...] = _dot(xn_ref[...], w_ref[...])


def _mlstm_in(x, g_mix, w_main, w_gates, layer, j_m, tm):
    t = x.shape[0]
    return pl.pallas_call(
        _mlstm_in_kernel,
        grid=(t // tm, M_MAIN // TN_PROJ),
        in_specs=[
            pl.BlockSpec((tm, D_MODEL), lambda i, j: (i, 0)),
            pl.BlockSpec((None, 1, D_MODEL), lambda i, j: (layer, 0, 0)),
            pl.BlockSpec((None, D_MODEL, TN_PROJ), lambda i, j: (j_m, 0, j)),
            pl.BlockSpec((None, D_MODEL, 128), lambda i, j: (j_m, 0, 0)),
        ],
        out_specs=[
            pl.BlockSpec((tm, TN_PROJ), lambda i, j: (i, j)),
            pl.BlockSpec((tm, 128), lambda i, j: (i, 0)),
        ],
        out_shape=[jax.ShapeDtypeStruct((t, M_MAIN), F32), jax.ShapeDtypeStruct((t, 128), F32)],
        scratch_shapes=[pltpu.VMEM((tm, D_MODEL), BF16)],
        compiler_params=_cparams(("parallel", "arbitrary")),
        name="mlstm_in",
    )(x, g_mix, w_main, w_gates)


_M_AUG = M_V_DIM + 128


def _cummax_lanes(x):
    n = x.shape[-1]
    lane = lax.broadcasted_iota(jnp.int32, x.shape, x.ndim - 1)
    if n % 128:
        out = x
        for s in range(n - 1):
            out = jnp.where(lane > s, jnp.maximum(out, x[:, s:s + 1]), out)
        return out
    shift = 1
    while shift < n:
        x = jnp.where(lane >= shift, jnp.maximum(x, pltpu.roll(x, shift, x.ndim - 1)), x)
        shift *= 2
    return x


def _mlstm_chunk_kernel(l_valid, zero_init, *refs):
    if zero_init:
        main_ref, gates_ref, bias_ref, gh_ref = refs[:4]
        rest = refs[4:]
    else:
        main_ref, gates_ref, bias_ref, gh_ref, c0_ref, n0_ref, m0_ref = refs[:7]
        rest = refs[7:]
    hn_ref, cout_ref, nout_ref, mout_ref, c_s, m_s = rest
    c = pl.program_id(1)
    L = main_ref.shape[0]
    scale_k = M_QK_DIM ** -0.5

    @pl.when(c == 0)
    def _():
        c_s[...] = jnp.zeros_like(c_s)
        if zero_init:
            m_s[...] = jnp.zeros_like(m_s)
        else:
            c_s[:, :, 0:M_V_DIM] = c0_ref[...]
            c_s[:, :, M_V_DIM:_M_AUG] = jnp.broadcast_to(n0_ref[...], (M_HEADS, M_QK_DIM, _M_AUG - M_V_DIM))
            for h in range(M_HEADS):
                m_s[h:h + 1, :] = jnp.broadcast_to(m0_ref[h], (1, 128))

    pre = gates_ref[...] + bias_ref[...]
    lane = lax.broadcasted_iota(jnp.int32, pre.shape, 1)
    act = jnp.where(lane < M_HEADS, IGATE_CAP * jnp.tanh(pre / IGATE_CAP), jax.nn.log_sigmoid(pre))
    r = lax.broadcasted_iota(jnp.int32, (L, L), 0)
    cc = lax.broadcasted_iota(jnp.int32, (L, L), 1)
    causal = cc <= r
    bcols = jnp.dot(causal.astype(F32), act, precision=HIGHEST, preferred_element_type=F32)
    act_t = act.T
    b_t = bcols.T
    rows = lax.broadcasted_iota(jnp.int32, (L, 1), 0)

    li_rows = act_t[0:M_HEADS, :]
    b_rows = b_t[M_HEADS:2 * M_HEADS, :]
    m_prev = m_s[:, 0:1]
    m_rows = b_rows + jnp.maximum(m_prev, _cummax_lanes(li_rows - b_rows))
    a_rows = jnp.exp(b_rows + m_prev - m_rows)
    stats = jnp.concatenate([m_rows, a_rows, jnp.exp(-m_rows),
                             jnp.zeros((128 - 3 * M_HEADS, L), F32)], axis=0).T
    ones_cols = jnp.ones((L, _M_AUG - M_V_DIM), BF16)

    for h in range(M_HEADS):
        q = main_ref[:, h * M_QK_DIM:(h + 1) * M_QK_DIM]
        k = main_ref[:, D_MQ + h * M_QK_DIM:D_MQ + (h + 1) * M_QK_DIM]
        v = main_ref[:, 2 * D_MQ + h * M_V_DIM:2 * D_MQ + (h + 1) * M_V_DIM]
        o = main_ref[:, 2 * D_MQ + D_MV + h * M_V_DIM:2 * D_MQ + D_MV + (h + 1) * M_V_DIM]
        b_col = bcols[:, M_HEADS + h:M_HEADS + h + 1]
        b_row = b_rows[h:h + 1, :]
        li_col = act[:, h:h + 1]
        li_row = li_rows[h:h + 1, :]
        m = stats[:, h:h + 1]
        a = stats[:, M_HEADS + h:M_HEADS + h + 1]
        exp_neg_m = stats[:, 2 * M_HEADS + h:2 * M_HEADS + h + 1]
        c0 = c_s[h]

        w = jnp.exp(jnp.where(causal, b_col - b_row + li_row, -jnp.inf) - m)
        qb = q.astype(BF16)
        kb = k.astype(BF16)
        v_aug = jnp.concatenate([v.astype(BF16), ones_cols], axis=1)
        s = _dot_nt(qb, kb) * scale_k * w
        nd = _dot(s.astype(BF16), v_aug) + a * _dot(qb, c0.astype(BF16))
        hh = nd[:, 0:M_V_DIM] / jnp.maximum(jnp.abs(nd[:, M_V_DIM:M_V_DIM + 1]), exp_neg_m)
        hn = hh * lax.rsqrt(jnp.mean(hh * hh, axis=-1, keepdims=True) + EPS)
        hn = hn * gh_ref[:, h * M_V_DIM:(h + 1) * M_V_DIM] * jax.nn.sigmoid(o)
        hn_ref[:, h * M_V_DIM:(h + 1) * M_V_DIM] = hn.astype(hn_ref.dtype)

        m_end = m[l_valid - 1:l_valid]
        a_end = a[l_valid - 1:l_valid]
        b_last = b_col[l_valid - 1:l_valid]
        wk = jnp.exp(b_last - b_col + li_col - m_end)
        if l_valid < L:
            wk = jnp.where(rows < l_valid, wk, 0.0)
        kw = k * (scale_k * wk)
        c_s[h] = a_end * c0 + _dot_tn(kw.astype(BF16), v_aug)
        m_s[h:h + 1, :] = jnp.broadcast_to(m_end, (1, 128))

    @pl.when(c == pl.num_programs(1) - 1)
    def _():
        cout_ref[...] = c_s[:, :, 0:M_V_DIM]
        nout_ref[...] = c_s[:, :, M_V_DIM:_M_AUG]
        mout_ref[...] = m_s[...]


def _mlstm_chunk(main, gates, bias, g_h, j_m, n_seq, n_chunks, chunk, l_valid, state=None):
    zero_init = state is None
    t = main.shape[0]
    in_specs = [
        pl.BlockSpec((chunk, M_MAIN), lambda n, c: (n * n_chunks + c, 0)),
        pl.BlockSpec((chunk, 128), lambda n, c: (n * n_chunks + c, 0)),
        pl.BlockSpec((None, 1, 128), lambda n, c: (j_m, 0, 0)),
        pl.BlockSpec((None, 1, D_MV), lambda n, c: (j_m, 0, 0)),
    ]
    args = [main, gates, bias, g_h]
    if not zero_init:
        in_specs += [
            pl.BlockSpec((None, None, M_HEADS, M_QK_DIM, M_V_DIM), lambda n, c: (j_m, n, 0, 0, 0)),
            pl.BlockSpec((None, None, M_HEADS, M_QK_DIM, 1), lambda n, c: (j_m, n, 0, 0, 0)),
            pl.BlockSpec((None, None, M_HEADS, 1, 1), lambda n, c: (j_m, n, 0, 0, 0)),
        ]
        args += list(state)
    hn_dtype = BF16 if chunk % 16 == 0 else F32
    return pl.pallas_call(
        functools.partial(_mlstm_chunk_kernel, l_valid, zero_init),
        grid=(n_seq, n_chunks),
        in_specs=in_specs,
        out_specs=[
            pl.BlockSpec((chunk, D_MV), lambda n, c: (n * n_chunks + c, 0)),
            pl.BlockSpec((None, M_HEADS, M_QK_DIM, M_V_DIM), lambda n, c: (n, 0, 0, 0)),
            pl.BlockSpec((None, M_HEADS, M_QK_DIM, _M_AUG - M_V_DIM), lambda n, c: (n, 0, 0, 0)),
            pl.BlockSpec((None, M_HEADS, 128), lambda n, c: (n, 0, 0)),
        ],
        out_shape=[
            jax.ShapeDtypeStruct((t, D_MV), hn_dtype),
            jax.ShapeDtypeStruct((n_seq, M_HEADS, M_QK_DIM, M_V_DIM), F32),
            jax.ShapeDtypeStruct((n_seq, M_HEADS, M_QK_DIM, _M_AUG - M_V_DIM), F32),
            jax.ShapeDtypeStruct((n_seq, M_HEADS, 128), F32),
        ],
        scratch_shapes=[
            pltpu.VMEM((M_HEADS, M_QK_DIM, _M_AUG), F32),
            pltpu.VMEM((M_HEADS, 128), F32),
        ],
        compiler_params=_cparams(("parallel", "arbitrary")),
        name="mlstm_chunk",
    )(*args)


def _sample_select_kernel(q_ref, psum_ref, sel_ref, kmean_ref):
    c_past = kmean_ref.shape[1]

    for c in range(c_past):
        tot = psum_ref[c * PAGES_PER_BLOCK]
        for pp in range(1, PAGES_PER_BLOCK):
            tot = tot + psum_ref[c * PAGES_PER_BLOCK + pp]
        tot = tot * (1.0 / MOBA_BLOCK)
        for h in range(N_HEADS):
            kmean_ref[h, c:c + 1, :] = tot[h:h + 1, :]

    out_lane = lax.broadcasted_iota(jnp.int32, sel_ref.shape, 1)
    out = jnp.zeros(sel_ref.shape, jnp.int32)
    for h in range(N_HEADS):
        g = _dot_nt(q_ref[:, h * HEAD_DIM:(h + 1) * HEAD_DIM], kmean_ref[h], precision=HIGHEST)
        lane = lax.broadcasted_iota(jnp.int32, g.shape, 1)
        for kk in range(MOBA_TOP_K):
            mx = jnp.max(g, axis=-1, keepdims=True)
            idx = jnp.min(jnp.where(g == mx, lane, c_past), axis=-1, keepdims=True)
            out = jnp.where(out_lane == h * MOBA_TOP_K + kk, idx, out)
            g = jnp.where(lane == idx, -jnp.inf, g)
    sel_ref[...] = out


def _sample_select(psum, q, n_db):
    n_pages = psum.shape[1]
    c_past = n_pages // PAGES_PER_BLOCK
    return pl.pallas_call(
        _sample_select_kernel,
        grid=(n_db,),
        in_specs=[
            pl.BlockSpec((SAMPLE_T_PAD, N_HEADS * HEAD_DIM), lambda db: (db, 0)),
            pl.BlockSpec((None, n_pages, N_HEADS, HEAD_DIM), lambda db: (db, 0, 0, 0)),
        ],
        out_specs=pl.BlockSpec((None, SAMPLE_T_PAD, 128), lambda db: (db, 0, 0)),
        out_shape=jax.ShapeDtypeStruct((n_db, SAMPLE_T_PAD, 128), jnp.int32),
        scratch_shapes=[pltpu.VMEM((N_HEADS, c_past, HEAD_DIM), F32)],
        compiler_params=_cparams(("parallel",)),
        name="sample_select",
    )(q, psum)


_N_SEL_PAGES = MOBA_TOP_K * PAGES_PER_BLOCK


def _sample_attn_kernel(t_valid, j_attn, pt_ref, sel_ref, q_ref, kn_ref, vn_ref, ck_ref, cv_ref, o_ref,
                        kbuf, vbuf, sem):
    n_heads = pl.num_programs(1)
    step = pl.program_id(0) * n_heads + pl.program_id(1)
    n_steps = pl.num_programs(0) * n_heads
    slot = step % 2
    scale = HEAD_DIM ** -0.5

    def copies(step_, slot_):
        db_ = step_ // n_heads
        h_ = step_ % n_heads
        out = []
        for t in range(t_valid):
            for kk in range(MOBA_TOP_K):
                blk = sel_ref[db_ * (SAMPLE_T_PAD * 128) + t * 128 + h_ * MOBA_TOP_K + kk]
                for pp in range(PAGES_PER_BLOCK):
                    page = pt_ref[db_, blk * PAGES_PER_BLOCK + pp]
                    i = (t * MOBA_TOP_K + kk) * PAGES_PER_BLOCK + pp
                    out.append(pltpu.make_async_copy(ck_ref.at[j_attn, page, :, h_, :], kbuf.at[slot_, i],
                                                     sem.at[0, slot_]))
                    out.append(pltpu.make_async_copy(cv_ref.at[j_attn, page, :, h_, :], vbuf.at[slot_, i],
                                                     sem.at[1, slot_]))
        return out

    def start_all(step_, slot_):
        for i, c in enumerate(copies(step_, slot_)):
            c.start(priority=i % 2)

    @pl.when(step == 0)
    def _():
        start_all(step, slot)

    @pl.when(step + 1 < n_steps)
    def _():
        start_all(step + 1, 1 - slot)

    for c in copies(step, slot):
        c.wait()

    kn = kn_ref[...].astype(BF16)
    vn = vn_ref[...].astype(BF16)
    lane = lax.broadcasted_iota(jnp.int32, (SAMPLE_T_PAD, SAMPLE_T_PAD), 1)
    rows = []
    for t in range(t_valid):
        q = jnp.broadcast_to(q_ref[t:t + 1, :], (SAMPLE_T_PAD, HEAD_DIM)).astype(BF16)
        ks = kbuf[slot, t * _N_SEL_PAGES:(t + 1) * _N_SEL_PAGES].reshape(_N_SEL_PAGES * PAGE_SIZE, HEAD_DIM)
        vs = vbuf[slot, t * _N_SEL_PAGES:(t + 1) * _N_SEL_PAGES].reshape(_N_SEL_PAGES * PAGE_SIZE, HEAD_DIM)
        s_sel = _dot_nt(q, ks.astype(BF16)) * scale
        s_own = jnp.where(lane <= t, _dot_nt(q, kn) * scale, -jnp.inf)
        m = jnp.maximum(jnp.max(s_sel, axis=-1, keepdims=True), jnp.max(s_own, axis=-1, keepdims=True))
        p_sel = jnp.exp(s_sel - m)
        p_own = jnp.exp(s_own - m)
        l = jnp.sum(p_sel, axis=-1, keepdims=True) + jnp.sum(p_own, axis=-1, keepdims=True)
        out = (_dot(p_sel.astype(BF16), vs.astype(BF16)) + _dot(p_own.astype(BF16), vn)) / l
        rows.append(out[0:1, :])
    rows.append(jnp.zeros((SAMPLE_T_PAD - t_valid, HEAD_DIM), F32))
    o_ref[...] = jnp.concatenate(rows, axis=0)


def _sample_attn(page_table, sel_flat, q, k_new, v_new, cache_k, cache_v, j_attn, n_db, t_valid):
    row_spec = pl.BlockSpec((SAMPLE_T_PAD, HEAD_DIM), lambda db, h, pt, sel: (db, h))
    hbm_spec = pl.BlockSpec(memory_space=pl.ANY)
    n_slices = t_valid * _N_SEL_PAGES
    grid_spec = pltpu.PrefetchScalarGridSpec(
        num_scalar_prefetch=2,
        grid=(n_db, N_HEADS),
        in_specs=[row_spec, row_spec, row_spec, hbm_spec, hbm_spec],
        out_specs=row_spec,
        scratch_shapes=[
            pltpu.VMEM((2, n_slices, PAGE_SIZE, HEAD_DIM), F32),
            pltpu.VMEM((2, n_slices, PAGE_SIZE, HEAD_DIM), F32),
            pltpu.SemaphoreType.DMA((2, 2)),
        ],
    )
    return pl.pallas_call(
        functools.partial(_sample_attn_kernel, t_valid, j_attn),
        grid_spec=grid_spec,
        out_shape=jax.ShapeDtypeStruct((n_db * SAMPLE_T_PAD, N_HEADS * HEAD_DIM), F32),
        compiler_params=_cparams(("arbitrary", "arbitrary")),
        name="sample_attn",
    )(page_table, sel_flat, q, k_new, v_new, cache_k, cache_v)


def _rope_tables(pos):
    half = HEAD_DIM // 2
    inv = ROPE_THETA ** (-jnp.arange(half, dtype=F32) / half)
    ang = pos.astype(F32)[:, None] * inv[None, :]
    cos = jnp.cos(ang)
    sin = jnp.sin(ang)
    return jnp.concatenate([cos, cos], axis=-1), jnp.concatenate([-sin, sin], axis=-1)


def kernel(x_prompt, x_sample, cache_k, cache_v, state_C, state_n, state_m, page_table, g_ffn1, w_ffn1_up, w_ffn1_down, g_mix, attn_w_qkv, attn_g_q, attn_g_k, attn_w_o, mlstm_w_in, mlstm_b_if, mlstm_g_h, mlstm_w_out, g_ffn2, w_ffn2_up, w_ffn2_down):
    b_, s_, d = x_prompt.shape
    db, t_s, _ = x_sample.shape
    assert d == D_MODEL and s_ % MOBA_BLOCK == 0 and s_ % TM_PROJ == 0 and s_ % M_CHUNK == 0 and t_s <= SAMPLE_T_PAD
    assert PAST_LEN % MOBA_BLOCK == 0 and page_table.shape[1] * PAGE_SIZE == PAST_LEN
    n_mlstm = mlstm_w_in.shape[0]

    ffn_w = [(w_ffn1_up[0].astype(BF16), w_ffn1_down[0].astype(BF16))]
    wqkv, wo = attn_w_qkv.astype(BF16), attn_w_o.astype(BF16)
    w_in_main = mlstm_w_in.astype(BF16)
    w_in_gates = jnp.pad(mlstm_w_in[:, :, M_MAIN:], ((0, 0), (0, 0), (0, 128 - 2 * M_HEADS))).astype(BF16)
    w_out = mlstm_w_out.astype(BF16)
    g1 = g_ffn1.reshape(DEPTH, 1, d)
    g2 = g_ffn2.reshape(DEPTH, 1, d)
    gm = g_mix.reshape(DEPTH, 1, d)
    gq = attn_g_q.reshape(-1, 1, HEAD_DIM)
    gk = attn_g_k.reshape(-1, 1, HEAD_DIM)
    b_if = jnp.pad(mlstm_b_if, ((0, 0), (0, 128 - 2 * M_HEADS))).reshape(n_mlstm, 1, 128)
    g_h = mlstm_g_h.reshape(n_mlstm, 1, D_MV)

    cos_p, sin_p = _rope_tables(jnp.arange(s_))
    cos_s, sin_s = _rope_tables(PAST_LEN + (jnp.arange(db * SAMPLE_T_PAD) % SAMPLE_T_PAD))

    st_n = state_n.reshape(n_mlstm, db, M_HEADS, M_QK_DIM, 1)
    st_m = state_m.reshape(n_mlstm, db, M_HEADS, 1, 1)

    xp = x_prompt.reshape(b_ * s_, d)
    xs = jnp.pad(x_sample, ((0, 0), (0, SAMPLE_T_PAD - t_s), (0, 0))).reshape(db * SAMPLE_T_PAD, d)
    tm_s = db * SAMPLE_T_PAD

    n_attn = cache_k.shape[0]
    pt_flat = page_table.reshape(-1)
    pages_per_layer = pt_flat.shape[0]
    pages_per_call = n_attn * pages_per_layer // (2 * DEPTH)
    calls_per_layer = pages_per_layer // pages_per_call
    assert pages_per_call * 2 * DEPTH == n_attn * pages_per_layer and calls_per_layer * pages_per_call == pages_per_layer
    psums = [[] for _ in range(n_attn)]

    def ffn_prompt(x, i, second):
        call = 2 * i + second
        j_attn = call // calls_per_layer
        if second:
            nxt = (w_ffn1_up, w_ffn1_down, i + 1) if i + 1 < DEPTH else (None, None, 0)
        else:
            nxt = (w_ffn2_up, w_ffn2_down, i)
        side = (pt_flat, cache_k, j_attn, (call % calls_per_layer) * pages_per_call, pages_per_call) + nxt
        y, ps, up_bf, down_bf = _ffn(x, g2 if second else g1, *ffn_w[call], i, TM_PROMPT, side=side)
        psums[j_attn].append(ps)
        ffn_w.append((up_bf, down_bf))
        return y

    kp_l, vp_l, ks_l, vs_l = [], [], [], []
    cp_l, np_l, mp_l, cs_l, ns_l, ms_l = [], [], [], [], [], []
    for i in range(DEPTH):
        xp = ffn_prompt(xp, i, 0)
        j = i // 2
        if i % 2 == 0:
            qp, kp, vp = _qkv(xp, gm, wqkv, gq, gk, cos_p, sin_p, i, j, TM_PROJ)
            op = _moba_prompt(qp, kp, vp, b_, s_)
            xp = _proj_res(op, wo, xp, j, TM_PROMPT)
            kp_l.append(kp.reshape(b_, s_ // PAGE_SIZE, PAGE_SIZE, N_HEADS, HEAD_DIM))
            vp_l.append(vp.reshape(b_, s_ // PAGE_SIZE, PAGE_SIZE, N_HEADS, HEAD_DIM))
        else:
            main, gates = _mlstm_in(xp, gm, w_in_main, w_in_gates, i, j, TM_PROJ)
            hn, c_p, n_p, m_p = _mlstm_chunk(main, gates, b_if, g_h, j, b_, s_ // M_CHUNK, M_CHUNK, M_CHUNK)
            xp = _proj_res(hn, w_out, xp, j, TM_PROMPT)
            cp_l.append(c_p)
            np_l.append(n_p[:, :, :, 0])
            mp_l.append(m_p[:, :, 0])
        xp = ffn_prompt(xp, i, 1)

    psum_l = [jnp.concatenate(p, axis=0).reshape(db, -1, N_HEADS, HEAD_DIM) for p in psums]
    for i in range(DEPTH):
        xs = _ffn(xs, g1, *ffn_w[2 * i], i, tm_s)
        j = i // 2
        if i % 2 == 0:
            qs, ks, vs = _qkv(xs, gm, wqkv, gq, gk, cos_s, sin_s, i, j, tm_s)
            sel = _sample_select(psum_l[j], qs, db)
            os_ = _sample_attn(page_table, sel.reshape(-1), qs, ks, vs, cache_k, cache_v, j, db, t_s)
            xs = _proj_res(os_, wo, xs, j, tm_s)
            ks_l.append(ks.reshape(db, SAMPLE_T_PAD, N_HEADS, HEAD_DIM)[:, :t_s])
            vs_l.append(vs.reshape(db, SAMPLE_T_PAD, N_HEADS, HEAD_DIM)[:, :t_s])
        else:
            main, gates = _mlstm_in(xs, gm, w_in_main, w_in_gates, i, j, tm_s)
            hn, c_s, n_s, m_s = _mlstm_chunk(main, gates, b_if, g_h, j, db, 1, SAMPLE_T_PAD, t_s,
                                             state=(state_C, st_n, st_m))
            xs = _proj_res(hn, w_out, xs, j, tm_s)
            cs_l.append(c_s)
            ns_l.append(n_s[:, :, :, 0])
            ms_l.append(m_s[:, :, 0])
        xs = _ffn(xs, g2, *ffn_w[2 * i + 1], i, tm_s)

    y_prompt = xp.reshape(b_, s_, d)
    y_sample = xs.reshape(db, SAMPLE_T_PAD, d)[:, :t_s]
    return (y_prompt, y_sample, jnp.stack(kp_l), jnp.stack(vp_l), jnp.stack(ks_l), jnp.stack(vs_l),
            jnp.stack(cp_l), jnp.stack(np_l), jnp.stack(mp_l), jnp.stack(cs_l), jnp.stack(ns_l), jnp.stack(ms_l))
```
